```python
import jax, jax.numpy as jnp
from jax import lax
import numpy as np

D_MODEL = 1024
BATCH = 32
SEQ = 2048
DEPTH = 4

N_MIXERS = 2
N_A_LAYERS = (DEPTH + 1) // 2
N_B_LAYERS = DEPTH // 2
D_FF = 2816
CHUNK = 128
A_INNER = D_MODEL
A_GROUPS = 8
A_GROUP_DIM = A_INNER // A_GROUPS
B_HEADS = 16
B_HEAD_DIM = D_MODEL // B_HEADS
Q_BLOCK = 128
RMS_EPS = 1e-6
LN_EPS = 1e-5

kernel_name = "hybrid_gmlp_stickbreak_macaron"


def rms_norm(x, g):
    xf = x.astype(jnp.float32)
    y = xf * lax.rsqrt(jnp.mean(xf * xf, axis=-1, keepdims=True) + RMS_EPS)
    return (y * g.astype(jnp.float32)).astype(x.dtype)


def layer_norm(x, g):
    xf = x.astype(jnp.float32)
    mu = jnp.mean(xf, axis=-1, keepdims=True)
    var = jnp.mean(jnp.square(xf - mu), axis=-1, keepdims=True)
    return ((xf - mu) * lax.rsqrt(var + LN_EPS) * g.astype(jnp.float32)).astype(x.dtype)


def swiglu(x, w_gate, w_up, w_down):
    return (jax.nn.silu(x @ w_gate) * (x @ w_up)) @ w_down


def chunked_gmlp(x, w_in, ln_g, w_s, b_s, w_out):
    bsz, seq, _ = x.shape
    h = jax.nn.gelu(x @ w_in, approximate=False)
    u, v = jnp.split(h, 2, axis=-1)
    v = layer_norm(v, ln_g)
    v = v.reshape(bsz, seq // CHUNK, CHUNK, A_GROUPS, A_GROUP_DIM)
    causal = jnp.tril(jnp.ones((CHUNK, CHUNK), dtype=bool))
    w = jnp.where(causal[None], w_s, 0.0).astype(v.dtype)
    bias = jnp.swapaxes(b_s, 0, 1)[None, None, :, :, None].astype(v.dtype)
    v = jnp.einsum('gts,bcsgd->bctgd', w, v) + bias
    y = u * v.reshape(bsz, seq, A_INNER)
    return y @ w_out


def stick_breaking_attention(x, w_qkv, w_o):
    bsz, seq, _ = x.shape
    qkv = (x @ w_qkv).reshape(bsz, seq, 3, B_HEADS, B_HEAD_DIM)
    q = jnp.moveaxis(qkv[:, :, 0], 1, 2)
    k = jnp.moveaxis(qkv[:, :, 1], 1, 2)
    v = jnp.moveaxis(qkv[:, :, 2], 1, 2)
    scale = B_HEAD_DIM ** -0.5
    outs = []
    for blk in range(seq // Q_BLOCK):
        q0 = blk * Q_BLOCK
        kend = q0 + Q_BLOCK
        qb = q[:, :, q0:kend].astype(jnp.float32)
        kb = k[:, :, :kend].astype(jnp.float32)
        vb = v[:, :, :kend]
        z = jnp.einsum('bhqd,bhkd->bhqk', qb, kb) * scale
        t_idx = q0 + jnp.arange(Q_BLOCK)[:, None]
        s_idx = jnp.arange(kend)[None, :]
        strict = s_idx < t_idx
        log_keep = jnp.where(strict, jax.nn.log_sigmoid(-z), 0.0)
        later = lax.cumsum(log_keep, axis=3, reverse=True) - log_keep
        log_a = jax.nn.log_sigmoid(z) + later
        a = jnp.where(strict, jnp.exp(log_a), 0.0)
        outs.append(jnp.einsum('bhqk,bhkd->bhqd', a.astype(vb.dtype), vb))
    o = jnp.concatenate(outs, axis=2)
    o = jnp.moveaxis(o, 1, 2).reshape(bsz, seq, B_HEADS * B_HEAD_DIM)
    return o @ w_o


def setup_inputs(seed: int = 0) -> dict:
    key = jax.random.key(seed)
    ks = jax.random.split(key, 16)
    f32 = jnp.float32

    def dense(k, shape, fan_in):
        return jax.random.normal(k, shape, f32) * (fan_in ** -0.5)

    x = jax.random.normal(ks[0], (BATCH, SEQ, D_MODEL), f32)
    norm_pre = 1.0 + 0.05 * jax.random.normal(ks[1], (DEPTH, 3, D_MODEL), f32)
    norm_post = 1.0 + 0.05 * jax.random.normal(ks[2], (DEPTH, 3, D_MODEL), f32)
    ffn_w_gate = dense(ks[3], (DEPTH, 2, D_MODEL, D_FF), D_MODEL)
    ffn_w_up = dense(ks[4], (DEPTH, 2, D_MODEL, D_FF), D_MODEL)
    ffn_w_down = dense(ks[5], (DEPTH, 2, D_FF, D_MODEL), D_FF)
    a_w_in = dense(ks[6], (N_A_LAYERS, D_MODEL, 2 * A_INNER), D_MODEL)
    a_ln_g = 1.0 + 0.05 * jax.random.normal(ks[7], (N_A_LAYERS, A_INNER), f32)
    a_w_s = dense(ks[8], (N_A_LAYERS, A_GROUPS, CHUNK, CHUNK), CHUNK)
    a_b_s = 1.0 + 0.1 * jax.random.normal(ks[9], (N_A_LAYERS, A_GROUPS, CHUNK), f32)
    a_w_out = dense(ks[10], (N_A_LAYERS, A_INNER, D_MODEL), A_INNER)
    b_w_qkv = dense(ks[11], (N_B_LAYERS, D_MODEL, 3 * D_MODEL), D_MODEL)
    b_w_o = dense(ks[12], (N_B_LAYERS, D_MODEL, D_MODEL), D_MODEL)
    return {"x": x, "norm_pre": norm_pre, "norm_post": norm_post,
            "ffn_w_gate": ffn_w_gate, "ffn_w_up": ffn_w_up, "ffn_w_down": ffn_w_down,
            "a_w_in": a_w_in, "a_ln_g": a_ln_g, "a_w_s": a_w_s, "a_b_s": a_b_s,
            "a_w_out": a_w_out, "b_w_qkv": b_w_qkv, "b_w_o": b_w_o}


def reference(x, norm_pre, norm_post, ffn_w_gate, ffn_w_up, ffn_w_down,
              a_w_in, a_ln_g, a_w_s, a_b_s, a_w_out, b_w_qkv, b_w_o):
    h = x
    for layer in range(DEPTH):
        f = swiglu(rms_norm(h, norm_pre[layer, 0]),
                   ffn_w_gate[layer, 0], ffn_w_up[layer, 0], ffn_w_down[layer, 0])
        h = h + 0.5 * rms_norm(f, norm_post[layer, 0])
        hn = rms_norm(h, norm_pre[layer, 1])
        if layer % N_MIXERS == 0:
            ia = layer // N_MIXERS
            m = chunked_gmlp(hn, a_w_in[ia], a_ln_g[ia], a_w_s[ia], a_b_s[ia], a_w_out[ia])
        else:
            ib = layer // N_MIXERS
            m = stick_breaking_attention(hn, b_w_qkv[ib], b_w_o[ib])
        h = h + rms_norm(m, norm_post[layer, 1])
        f = swiglu(rms_norm(h, norm_pre[layer, 2]),
                   ffn_w_gate[layer, 1], ffn_w_up[layer, 1], ffn_w_down[layer, 1])
        h = h + 0.5 * rms_norm(f, norm_post[layer, 2])
    return h
```

```python
import functools
import math

import jax
import jax.numpy as jnp
from jax import lax
from jax.experimental import pallas as pl
from jax.experimental.pallas import tpu as pltpu

RMS_EPS = 1e-6
LN_EPS = 1e-5
CHUNK = 128
A_GROUPS = 8
B_HEADS = 16
HEAD_DIM = 64
Q_BLOCK = 128
HEADS_PER_STEP = 2
FF_CHUNK = 256
TOKEN_TILE = 512
VMEM_LIMIT_BYTES = 56 * 1024 * 1024

F32 = jnp.float32
BF16 = jnp.bfloat16


def _rms(x, g):
    ms = jnp.mean(x * x, axis=-1, keepdims=True)
    return x * lax.rsqrt(ms + RMS_EPS) * g


def _dot(a, b):
    return jnp.dot(a, b, preferred_element_type=F32)


def _dot_nt(a, b):
    return lax.dot_general(a, b, (((1,), (1,)), ((), ())), preferred_element_type=F32)


def _params():
    return pltpu.CompilerParams(
        dimension_semantics=("parallel",), vmem_limit_bytes=VMEM_LIMIT_BYTES)


def _resident():
    return pl.BlockSpec(memory_space=pltpu.VMEM)


def _row_spec(d):
    return pl.BlockSpec((1, d), lambda *_: (0, 0))


def _ffn_kernel(h_ref, pre_ref, post_ref, wg_ref, wu_ref, wd_ref, o_ref, act_ref):
    x = h_ref[...]
    xn = _rms(x, pre_ref[...]).astype(BF16)
    d_ff = wg_ref.shape[1]
    for c in range(d_ff // FF_CHUNK):
        sl = slice(c * FF_CHUNK, (c + 1) * FF_CHUNK)
        g = _dot(xn, wg_ref[:, sl])
        u = _dot(xn, wu_ref[:, sl])
        act_ref[:, sl] = (g * jax.nn.sigmoid(g) * u).astype(BF16)
    f = _dot(act_ref[...], wd_ref[...])
    o_ref[...] = x + 0.5 * _rms(f, post_ref[...])


def _ffn(h, pre_g, post_g, wg, wu, wd):
    m, d = h.shape
    d_ff = wg.shape[1]
    tm = TOKEN_TILE
    tile = pl.BlockSpec((tm, d), lambda i: (i, 0))
    return pl.pallas_call(
        _ffn_kernel,
        grid=(m // tm,),
        in_specs=[tile, _row_spec(d), _row_spec(d), _resident(), _resident(), _resident()],
        out_specs=tile,
        out_shape=jax.ShapeDtypeStruct((m, d), F32),
        scratch_shapes=[pltpu.VMEM((tm, d_ff), BF16)],
        compiler_params=_params(),
        name="ffn",
    )(h, pre_g, post_g, wg, wu, wd)


def _gelu(x):
    return 0.5 * x * (1.0 + lax.erf(x * math.sqrt(0.5)))


def _mixa_kernel(h_ref, pre_ref, post_ref, win_ref, lng_ref, ws_ref, bias_ref, wout_ref,
                 o_ref, u_ref, v_ref, y_ref):
    x = h_ref[...]
    tm = x.shape[0]
    inner = wout_ref.shape[0]
    gdim = inner // A_GROUPS
    xn = _rms(x, pre_ref[...]).astype(BF16)
    u_ref[...] = _gelu(_dot(xn, win_ref[:, :inner]))
    v = _gelu(_dot(xn, win_ref[:, inner:]))
    mu = jnp.mean(v, axis=-1, keepdims=True)
    vc = v - mu
    var = jnp.mean(vc * vc, axis=-1, keepdims=True)
    v_ref[...] = (vc * lax.rsqrt(var + LN_EPS) * lng_ref[...]).astype(BF16)
    t_idx = lax.broadcasted_iota(jnp.int32, (CHUNK, CHUNK), 0)
    s_idx = lax.broadcasted_iota(jnp.int32, (CHUNK, CHUNK), 1)
    causal = s_idx <= t_idx
    for g in range(A_GROUPS):
        cols = slice(g * gdim, (g + 1) * gdim)
        w = jnp.where(causal, ws_ref[g], 0.0).astype(BF16)
        for c in range(tm // CHUNK):
            rows = slice(c * CHUNK, (c + 1) * CHUNK)
            gated = _dot(w, v_ref[rows, cols]) + bias_ref[:, cols]
            y_ref[rows, cols] = (u_ref[rows, cols] * gated).astype(BF16)
    mix = _dot(y_ref[...], wout_ref[...])
    o_ref[...] = x + _rms(mix, post_ref[...])


def _mixer_a(h, pre_g, post_g, w_in, ln_g, w_s, bias, w_out):
    m, d = h.shape
    inner = w_out.shape[0]
    tm = TOKEN_TILE
    tile = pl.BlockSpec((tm, d), lambda i: (i, 0))
    return pl.pallas_call(
        _mixa_kernel,
        grid=(m // tm,),
        in_specs=[tile, _row_spec(d), _row_spec(d), _resident(), _row_spec(inner),
                  _resident(), _resident(), _resident()],
        out_specs=tile,
        out_shape=jax.ShapeDtypeStruct((m, d), F32),
        scratch_shapes=[pltpu.VMEM((tm, inner), F32), pltpu.VMEM((tm, inner), BF16),
                        pltpu.VMEM((tm, inner), BF16)],
        compiler_params=_params(),
        name="mixer_a",
    )(h, pre_g, post_g, w_in, ln_g, w_s, bias, w_out)


def _qkv_kernel(h_ref, pre_ref, wq_ref, wk_ref, wvt_ref, q_ref, k_ref, vt_ref):
    xn = _rms(h_ref[...], pre_ref[...]).astype(BF16)
    q_ref[...] = (_dot(xn, wq_ref[...]) * (HEAD_DIM ** -0.5)).astype(BF16)
    k_ref[...] = _dot(xn, wk_ref[...]).astype(BF16)
    vt_ref[...] = _dot_nt(wvt_ref[...], xn).astype(BF16)


def _qkv(h, pre_g, wq, wk, wvt, bsz, seq):
    m, d = h.shape
    tm = TOKEN_TILE
    nt = seq // tm
    tile = pl.BlockSpec((tm, d), lambda b, i: (b * nt + i, 0))
    return pl.pallas_call(
        _qkv_kernel,
        grid=(bsz, nt),
        in_specs=[tile, _row_spec(d), _resident(), _resident(), _resident()],
        out_specs=[tile, tile, pl.BlockSpec((None, d, tm), lambda b, i: (b, 0, i))],
        out_shape=[jax.ShapeDtypeStruct((m, d), BF16), jax.ShapeDtypeStruct((m, d), BF16),
                   jax.ShapeDtypeStruct((bsz, d, seq), BF16)],
        compiler_params=pltpu.CompilerParams(
            dimension_semantics=("parallel", "parallel"), vmem_limit_bytes=VMEM_LIMIT_BYTES),
        name="qkv",
    )(h, pre_g, wq, wk, wvt)


def _attn_kernel(q_ref, k_ref, vt_ref, o_ref):
    qi = pl.program_id(2)
    q = q_ref[...]
    lane = lax.broadcasted_iota(jnp.int32, q.shape, 1)
    zero = jnp.zeros_like(q)
    q_heads = [jnp.where(lane < HEAD_DIM, q, zero), jnp.where(lane >= HEAD_DIM, q, zero)]
    row = lax.broadcasted_iota(jnp.int32, (Q_BLOCK, Q_BLOCK), 0)
    col = lax.broadcasted_iota(jnp.int32, (Q_BLOCK, Q_BLOCK), 1)
    strict = row < col
    upper = jnp.where(strict, 1.0, 0.0).astype(BF16)
    upper2 = jnp.concatenate([upper, upper], axis=1)

    def block(j, state, diagonal):
        k_blk = k_ref[pl.ds(pl.multiple_of(j * Q_BLOCK, Q_BLOCK), Q_BLOCK), :]
        new_state = []
        for hd in range(HEADS_PER_STEP):
            carry, acc = state[hd]
            z = _dot_nt(k_blk, q_heads[hd])
            soft = jnp.log1p(jnp.exp(-jnp.abs(z)))
            log_beta = jnp.minimum(z, 0.0) - soft
            log_keep = -jnp.maximum(z, 0.0) - soft
            if diagonal:
                log_keep = jnp.where(strict, log_keep, 0.0)
            keep_hi = log_keep.astype(BF16)
            keep_lo = (log_keep - keep_hi.astype(F32)).astype(BF16)
            later = _dot(upper2, jnp.concatenate([keep_hi, keep_lo], axis=0))
            a = jnp.exp(log_beta + later + carry)
            if diagonal:
                a = jnp.where(strict, a, 0.0)
            vt_blk = vt_ref[hd * HEAD_DIM:(hd + 1) * HEAD_DIM,
                            pl.ds(pl.multiple_of(j * Q_BLOCK, Q_BLOCK), Q_BLOCK)]
            acc = acc + _dot(vt_blk, a.astype(BF16))
            carry = carry + later[0:1, :] + log_keep[0:1, :]
            new_state.append((carry, acc))
        return tuple(new_state)

    init = tuple((jnp.zeros((1, Q_BLOCK), F32), jnp.zeros((HEAD_DIM, Q_BLOCK), F32))
                 for _ in range(HEADS_PER_STEP))
    state = block(qi, init, True)
    state = lax.fori_loop(0, qi, lambda i, st: block(qi - 1 - i, st, False), state)
    out_t = jnp.concatenate([state[0][1], state[1][1]], axis=0)
    o_ref[...] = out_t.T.astype(BF16)


def _attention(q, k, vt, bsz, seq):
    m, d = q.shape
    nq = seq // Q_BLOCK
    width = HEADS_PER_STEP * HEAD_DIM
    q_spec = pl.BlockSpec((Q_BLOCK, width), lambda b, p, i: (b * nq + i, p))
    return pl.pallas_call(
        _attn_kernel,
        grid=(bsz, d // width, nq),
        in_specs=[q_spec,
                  pl.BlockSpec((seq, width), lambda b, p, i: (b, p)),
                  pl.BlockSpec((None, width, seq), lambda b, p, i: (b, p, 0))],
        out_specs=q_spec,
        out_shape=jax.ShapeDtypeStruct((m, d), BF16),
        compiler_params=pltpu.CompilerParams(
            dimension_semantics=("parallel", "parallel", "arbitrary"),
            vmem_limit_bytes=VMEM_LIMIT_BYTES),
        name="stickbreak",
    )(q, k, vt)


def _oproj_kernel(h_ref, a_ref, post_ref, wo_ref, o_ref):
    mix = _dot(a_ref[...], wo_ref[...])
    o_ref[...] = h_ref[...] + _rms(mix, post_ref[...])


def _oproj(h, attn, post_g, wo):
    m, d = h.shape
    tm = TOKEN_TILE
    tile = pl.BlockSpec((tm, d), lambda i: (i, 0))
    return pl.pallas_call(
        _oproj_kernel,
        grid=(m // tm,),
        in_specs=[tile, tile, _row_spec(d), _resident()],
        out_specs=tile,
        out_shape=jax.ShapeDtypeStruct((m, d), F32),
        compiler_params=_params(),
        name="attn_out",
    )(h, attn, post_g, wo)


def kernel(x, norm_pre, norm_post, ffn_w_gate, ffn_w_up, ffn_w_down, a_w_in, a_ln_g, a_w_s,
           a_b_s, a_w_out, b_w_qkv, b_w_o):
    bsz, seq, d = x.shape
    depth = norm_pre.shape[0]
    h = x.reshape(bsz * seq, d)
    wg, wu, wd = (w.astype(BF16) for w in (ffn_w_gate, ffn_w_up, ffn_w_down))
    a_in, a_out = a_w_in.astype(BF16), a_w_out.astype(BF16)
    b_qkv, b_o = b_w_qkv.astype(BF16), b_w_o.astype(BF16)
    gdim = a_w_out.shape[1] // A_GROUPS
    for layer in range(depth):
        pre = norm_pre[layer][:, None, :]
        post = norm_post[layer][:, None, :]
        h = _ffn(h, pre[0], post[0], wg[layer, 0], wu[layer, 0], wd[layer, 0])
        if layer % 2 == 0:
            ia = layer // 2
            bias = jnp.repeat(a_b_s[ia].T, gdim, axis=1)
            h = _mixer_a(h, pre[1], post[1], a_in[ia], a_ln_g[ia][None, :], a_w_s[ia], bias,
                         a_out[ia])
        else:
            ib = layer // 2
            w = b_qkv[ib]
            q, k, vt = _qkv(h, pre[1], w[:, :d], w[:, d:2 * d], w[:, 2 * d:].T, bsz, seq)
            attn = _attention(q, k, vt, bsz, seq)
            h = _oproj(h, attn, post[1], b_o[ib])
        h = _ffn(h, pre[2], post[2], wg[layer, 1], wu[layer, 1], wd[layer, 1])
    return h.reshape(bsz, seq, d)
```

```python
import math

import jax
import jax.numpy as jnp
from jax import lax
from jax.experimental import pallas as pl
from jax.experimental.pallas import tpu as pltpu

RMS_EPS = 1e-6
LN_EPS = 1e-5
CHUNK = 128
A_GROUPS = 8
HEAD_DIM = 64
K_BLOCK = 128
Q_BLOCK = 2 * K_BLOCK
PAIR = 2 * HEAD_DIM
PAIRS_PER_STEP = 2
SIGN_BIT = 0x80000000
FF_CHUNK = 256
TOKEN_TILE = 512
VMEM_LIMIT_BYTES = 56 * 1024 * 1024

F32 = jnp.float32
BF16 = jnp.bfloat16


def _rms(x, g):
    ms = jnp.mean(x * x, axis=-1, keepdims=True)
    return x * lax.rsqrt(ms + RMS_EPS) * g


def _dot(a, b):
    return jnp.dot(a, b, preferred_element_type=F32)


def _dot_nt(a, b):
    return lax.dot_general(a, b, (((1,), (1,)), ((), ())), preferred_element_type=F32)


def _params():
    return pltpu.CompilerParams(
        dimension_semantics=("parallel",), vmem_limit_bytes=VMEM_LIMIT_BYTES)


def _resident():
    return pl.BlockSpec(memory_space=pltpu.VMEM)


def _row_spec(d):
    return pl.BlockSpec((1, d), lambda *_: (0, 0))


def _ffn_kernel(h_ref, pre_ref, post_ref, wg_ref, wu_ref, wd_ref, o_ref, act_ref):
    x = h_ref[...]
    xn = _rms(x, pre_ref[...]).astype(BF16)
    d_ff = wg_ref.shape[1]
    for c in range(d_ff // FF_CHUNK):
        sl = slice(c * FF_CHUNK, (c + 1) * FF_CHUNK)
        g = _dot(xn, wg_ref[:, sl])
        u = _dot(xn, wu_ref[:, sl])
        act_ref[:, sl] = (g * jax.nn.sigmoid(g) * u).astype(BF16)
    f = _dot(act_ref[...], wd_ref[...])
    o_ref[...] = x + 0.5 * _rms(f, post_ref[...])


def _ffn(h, pre_g, post_g, wg, wu, wd):
    m, d = h.shape
    d_ff = wg.shape[1]
    tm = TOKEN_TILE
    tile = pl.BlockSpec((tm, d), lambda i: (i, 0))
    return pl.pallas_call(
        _ffn_kernel,
        grid=(m // tm,),
        in_specs=[tile, _row_spec(d), _row_spec(d), _resident(), _resident(), _resident()],
        out_specs=tile,
        out_shape=jax.ShapeDtypeStruct((m, d), F32),
        scratch_shapes=[pltpu.VMEM((tm, d_ff), BF16)],
        compiler_params=_params(),
        name="ffn",
    )(h, pre_g, post_g, wg, wu, wd)


def _gelu(x):
    return 0.5 * x * (1.0 + lax.erf(x * math.sqrt(0.5)))


def _mixa_kernel(h_ref, pre_ref, post_ref, win_ref, lng_ref, ws_ref, bias_ref, wout_ref,
                 o_ref, u_ref, v_ref, y_ref):
    x = h_ref[...]
    tm = x.shape[0]
    inner = wout_ref.shape[0]
    gdim = inner // A_GROUPS
    xn = _rms(x, pre_ref[...]).astype(BF16)
    u_ref[...] = _gelu(_dot(xn, win_ref[:, :inner]))
    v = _gelu(_dot(xn, win_ref[:, inner:]))
    mu = jnp.mean(v, axis=-1, keepdims=True)
    vc = v - mu
    var = jnp.mean(vc * vc, axis=-1, keepdims=True)
    v_ref[...] = (vc * lax.rsqrt(var + LN_EPS) * lng_ref[...]).astype(BF16)
    t_idx = lax.broadcasted_iota(jnp.int32, (CHUNK, CHUNK), 0)
    s_idx = lax.broadcasted_iota(jnp.int32, (CHUNK, CHUNK), 1)
    causal = s_idx <= t_idx
    for g in range(A_GROUPS):
        cols = slice(g * gdim, (g + 1) * gdim)
        w = jnp.where(causal, ws_ref[g], 0.0).astype(BF16)
        for c in range(tm // CHUNK):
            rows = slice(c * CHUNK, (c + 1) * CHUNK)
            gated = _dot(w, v_ref[rows, cols]) + bias_ref[:, cols]
            y_ref[rows, cols] = (u_ref[rows, cols] * gated).astype(BF16)
    mix = _dot(y_ref[...], wout_ref[...])
    o_ref[...] = x + _rms(mix, post_ref[...])


def _mixer_a(h, pre_g, post_g, w_in, ln_g, w_s, bias, w_out):
    m, d = h.shape
    inner = w_out.shape[0]
    tm = TOKEN_TILE
    tile = pl.BlockSpec((tm, d), lambda i: (i, 0))
    return pl.pallas_call(
        _mixa_kernel,
        grid=(m // tm,),
        in_specs=[tile, _row_spec(d), _row_spec(d), _resident(), _row_spec(inner),
                  _resident(), _resident(), _resident()],
        out_specs=tile,
        out_shape=jax.ShapeDtypeStruct((m, d), F32),
        scratch_shapes=[pltpu.VMEM((tm, inner), F32), pltpu.VMEM((tm, inner), BF16),
                        pltpu.VMEM((tm, inner), BF16)],
        compiler_params=_params(),
        name="mixer_a",
    )(h, pre_g, post_g, w_in, ln_g, w_s, bias, w_out)


def _qkv_kernel(h_ref, pre_ref, wqt_ref, wk_ref, wvt_ref, qt_ref, k_ref, vt_ref):
    xn = _rms(h_ref[...], pre_ref[...]).astype(BF16)
    qt_ref[...] = (_dot_nt(wqt_ref[...], xn) * (HEAD_DIM ** -0.5)).astype(BF16)
    k_ref[...] = _dot(xn, wk_ref[...]).astype(BF16)
    vt_ref[...] = _dot_nt(wvt_ref[...], xn).astype(BF16)


def _qkv(h, pre_g, wqt, wk, wvt, bsz, seq):
    m, d = h.shape
    tm = TOKEN_TILE
    nt = seq // tm
    tile = pl.BlockSpec((tm, d), lambda b, i: (b * nt + i, 0))
    tile_t = pl.BlockSpec((None, d, tm), lambda b, i: (b, 0, i))
    return pl.pallas_call(
        _qkv_kernel,
        grid=(bsz, nt),
        in_specs=[tile, _row_spec(d), _resident(), _resident(), _resident()],
        out_specs=[tile_t, tile, tile_t],
        out_shape=[jax.ShapeDtypeStruct((bsz, d, seq), BF16), jax.ShapeDtypeStruct((m, d), BF16),
                   jax.ShapeDtypeStruct((bsz, d, seq), BF16)],
        compiler_params=pltpu.CompilerParams(
            dimension_semantics=("parallel", "parallel"), vmem_limit_bytes=VMEM_LIMIT_BYTES),
        name="qkv",
    )(h, pre_g, wqt, wk, wvt)


def _attn_kernel(qt_ref, k_ref, vt_ref, o_ref, acc_ref):
    qi = pl.program_id(2)
    n_heads = 2 * PAIRS_PER_STEP
    row = lax.broadcasted_iota(jnp.int32, (K_BLOCK, Q_BLOCK), 0)
    col = lax.broadcasted_iota(jnp.int32, (K_BLOCK, Q_BLOCK), 1)
    upper = jnp.where(row < col, 1.0, 0.0).astype(BF16)[:, :K_BLOCK]
    upper2 = jnp.concatenate([upper, upper], axis=1)
    lane = lax.broadcasted_iota(jnp.int32, (K_BLOCK, PAIR), 1)
    first_head = lane < HEAD_DIM
    acc_ref[...] = jnp.zeros_like(acc_ref)

    def blocks(js, stricts, carries):
        rows = [pl.ds(pl.multiple_of(j * K_BLOCK, K_BLOCK), K_BLOCK) for j in js]
        z = {}
        for b in range(len(js)):
            for p in range(PAIRS_PER_STEP):
                k_pair = k_ref[rows[b], p * PAIR:(p + 1) * PAIR]
                zero = jnp.zeros_like(k_pair)
                k_split = jnp.concatenate([jnp.where(first_head, k_pair, zero),
                                           jnp.where(first_head, zero, k_pair)], axis=0)
                z_both = _dot(k_split, qt_ref[p * PAIR:(p + 1) * PAIR, :])
                for e in range(2):
                    z[b, 2 * p + e] = z_both[e * K_BLOCK:(e + 1) * K_BLOCK]
        log_beta, log_keep, split = {}, {}, {}
        for key, zz in z.items():
            neg_abs = pltpu.bitcast(pltpu.bitcast(zz, jnp.uint32) | jnp.uint32(SIGN_BIT), F32)
            soft = jnp.log(1.0 + jnp.exp(neg_abs))
            log_beta[key] = jnp.minimum(zz, 0.0) - soft
            keep = log_beta[key] - zz
            if stricts[key[0]] is not None:
                keep = jnp.where(stricts[key[0]], keep, 0.0)
            keep_hi = keep.astype(BF16)
            keep_lo = (keep - keep_hi.astype(F32)).astype(BF16)
            log_keep[key] = keep[0:1, :]
            split[key] = jnp.concatenate([keep_hi, keep_lo], axis=0)
        later = {key: _dot(upper2, s) for key, s in split.items()}
        carries = list(carries)
        a = {}
        for key in z:
            b, hd = key
            w = jnp.exp(log_beta[key] + later[key] + carries[hd])
            if stricts[b] is not None:
                w = jnp.where(stricts[b], w, 0.0)
            a[key] = w.astype(BF16)
            carries[hd] = carries[hd] + later[key][0:1, :] + log_keep[key]
        for hd in range(n_heads):
            hrows = slice(hd * HEAD_DIM, (hd + 1) * HEAD_DIM)
            total = acc_ref[hrows, :]
            for b in range(len(js)):
                total = total + _dot(vt_ref[hrows, rows[b]], a[b, hd])
            acc_ref[hrows, :] = total
        return tuple(carries)

    carries = tuple(jnp.zeros((1, Q_BLOCK), F32) for _ in range(n_heads))
    carries = blocks([2 * qi + 1, 2 * qi], [row + K_BLOCK < col, row < col], carries)

    def two_blocks(i, carries):
        j = 2 * qi - 1 - 2 * i
        return blocks([j, j - 1], [None, None], carries)

    lax.fori_loop(0, qi, two_blocks, carries)
    o_ref[...] = acc_ref[...].T.astype(BF16)


def _attention(qt, k, vt, bsz, seq):
    m, d = k.shape
    nq = seq // Q_BLOCK
    width = PAIRS_PER_STEP * PAIR
    return pl.pallas_call(
        _attn_kernel,
        grid=(bsz, d // width, nq),
        in_specs=[pl.BlockSpec((None, width, Q_BLOCK), lambda b, g, i: (b, g, i)),
                  pl.BlockSpec((seq, width), lambda b, g, i: (b, g)),
                  pl.BlockSpec((None, width, seq), lambda b, g, i: (b, g, 0))],
        out_specs=pl.BlockSpec((Q_BLOCK, width), lambda b, g, i: (b * nq + i, g)),
        out_shape=jax.ShapeDtypeStruct((m, d), BF16),
        scratch_shapes=[pltpu.VMEM((width, Q_BLOCK), F32)],
        compiler_params=pltpu.CompilerParams(
            dimension_semantics=("parallel", "parallel", "arbitrary"),
            vmem_limit_bytes=VMEM_LIMIT_BYTES),
        name="stickbreak",
    )(qt, k, vt)


def _oproj_kernel(h_ref, a_ref, post_ref, wo_ref, o_ref):
    mix = _dot(a_ref[...], wo_ref[...])
    o_ref[...] = h_ref[...] + _rms(mix, post_ref[...])


def _oproj(h, attn, post_g, wo):
    m, d = h.shape
    tm = TOKEN_TILE
    tile = pl.BlockSpec((tm, d), lambda i: (i, 0))
    return pl.pallas_call(
        _oproj_kernel,
        grid=(m // tm,),
        in_specs=[tile, tile, _row_spec(d), _resident()],
        out_specs=tile,
        out_shape=jax.ShapeDtypeStruct((m, d), F32),
        compiler_params=_params(),
        name="attn_out",
    )(h, attn, post_g, wo)


def kernel(x, norm_pre, norm_post, ffn_w_gate, ffn_w_up, ffn_w_down, a_w_in, a_ln_g, a_w_s,
           a_b_s, a_w_out, b_w_qkv, b_w_o):
    bsz, seq, d = x.shape
    depth = norm_pre.shape[0]
    h = x.reshape(bsz * seq, d)
    wg, wu, wd = (w.astype(BF16) for w in (ffn_w_gate, ffn_w_up, ffn_w_down))
    a_in, a_out = a_w_in.astype(BF16), a_w_out.astype(BF16)
    b_qkv, b_o = b_w_qkv.astype(BF16), b_w_o.astype(BF16)
    gdim = a_w_out.shape[1] // A_GROUPS
    for layer in range(depth):
        pre = norm_pre[layer][:, None, :]
        post = norm_post[layer][:, None, :]
        h = _ffn(h, pre[0], post[0], wg[layer, 0], wu[layer, 0], wd[layer, 0])
        if layer % 2 == 0:
            ia = layer // 2
            bias = jnp.repeat(a_b_s[ia].T, gdim, axis=1)
            h = _mixer_a(h, pre[1], post[1], a_in[ia], a_ln_g[ia][None, :], a_w_s[ia], bias,
                         a_out[ia])
        else:
            ib = layer // 2
            w = b_qkv[ib]
            qt, k, vt = _qkv(h, pre[1], w[:, :d].T, w[:, d:2 * d], w[:, 2 * d:].T, bsz, seq)
            attn = _attention(qt, k, vt, bsz, seq)
            h = _oproj(h, attn, post[1], b_o[ib])
        h = _ffn(h, pre[2], post[2], wg[layer, 1], wu[layer, 1], wd[layer, 1])
    return h.reshape(bsz, seq, d)
```

```python
import math

import jax
import jax.numpy as jnp
from jax import lax
from jax.experimental import pallas as pl
from jax.experimental.pallas import tpu as pltpu

RMS_EPS = 1e-6
LN_EPS = 1e-5
CHUNK = 128
A_GROUPS = 8
HEAD_DIM = 64
K_BLOCK = 128
Q_BLOCK = 2 * K_BLOCK
PAIR = 2 * HEAD_DIM
PAIRS_PER_STEP = 2
SIGN_BIT = 0x80000000
FF_CHUNK = 256
TOKEN_TILE = 512
VMEM_LIMIT_BYTES = 56 * 1024 * 1024

F32 = jnp.float32
BF16 = jnp.bfloat16


def _rms(x, g):
    ms = jnp.mean(x * x, axis=-1, keepdims=True)
    return x * lax.rsqrt(ms + RMS_EPS) * g


def _dot(a, b):
    return jnp.dot(a, b, preferred_element_type=F32)


def _dot_nt(a, b):
    return lax.dot_general(a, b, (((1,), (1,)), ((), ())), preferred_element_type=F32)


def _params():
    return pltpu.CompilerParams(
        dimension_semantics=("parallel",), vmem_limit_bytes=VMEM_LIMIT_BYTES)


def _resident():
    return pl.BlockSpec(memory_space=pltpu.VMEM)


def _row_spec(d):
    return pl.BlockSpec((1, d), lambda *_: (0, 0))


def _ffn_kernel(h_ref, pre_ref, post_ref, wg_ref, wu_ref, wd_ref, o_ref, act_ref):
    x = h_ref[...]
    xn = _rms(x, pre_ref[...]).astype(BF16)
    d_ff = wg_ref.shape[1]
    for c in range(d_ff // FF_CHUNK):
        sl = slice(c * FF_CHUNK, (c + 1) * FF_CHUNK)
        g = _dot(xn, wg_ref[:, sl])
        u = _dot(xn, wu_ref[:, sl])
        act_ref[:, sl] = (g * jax.nn.sigmoid(g) * u).astype(BF16)
    f = _dot(act_ref[...], wd_ref[...])
    o_ref[...] = x + 0.5 * _rms(f, post_ref[...])


def _ffn(h, pre_g, post_g, wg, wu, wd):
    m, d = h.shape
    d_ff = wg.shape[1]
    tm = TOKEN_TILE
    tile = pl.BlockSpec((tm, d), lambda i: (i, 0))
    return pl.pallas_call(
        _ffn_kernel,
        grid=(m // tm,),
        in_specs=[tile, _row_spec(d), _row_spec(d), _resident(), _resident(), _resident()],
        out_specs=tile,
        out_shape=jax.ShapeDtypeStruct((m, d), F32),
        scratch_shapes=[pltpu.VMEM((tm, d_ff), BF16)],
        compiler_params=_params(),
        name="ffn",
    )(h, pre_g, post_g, wg, wu, wd)


def _gelu(x):
    return 0.5 * x * (1.0 + lax.erf(x * math.sqrt(0.5)))


def _mixa_kernel(h_ref, pre_ref, post_ref, win_ref, lng_ref, ws_ref, bias_ref, wout_ref,
                 o_ref, u_ref, v_ref, y_ref):
    x = h_ref[...]
    tm = x.shape[0]
    inner = wout_ref.shape[0]
    gdim = inner // A_GROUPS
    xn = _rms(x, pre_ref[...]).astype(BF16)
    u_ref[...] = _gelu(_dot(xn, win_ref[:, :inner]))
    v = _gelu(_dot(xn, win_ref[:, inner:]))
    mu = jnp.mean(v, axis=-1, keepdims=True)
    vc = v - mu
    var = jnp.mean(vc * vc, axis=-1, keepdims=True)
    v_ref[...] = (vc * lax.rsqrt(var + LN_EPS) * lng_ref[...]).astype(BF16)
    t_idx = lax.broadcasted_iota(jnp.int32, (CHUNK, CHUNK), 0)
    s_idx = lax.broadcasted_iota(jnp.int32, (CHUNK, CHUNK), 1)
    causal = s_idx <= t_idx
    for g in range(A_GROUPS):
        cols = slice(g * gdim, (g + 1) * gdim)
        w = jnp.where(causal, ws_ref[g], 0.0).astype(BF16)
        for c in range(tm // CHUNK):
            rows = slice(c * CHUNK, (c + 1) * CHUNK)
            gated = _dot(w, v_ref[rows, cols]) + bias_ref[:, cols]
            y_ref[rows, cols] = (u_ref[rows, cols] * gated).astype(BF16)
    mix = _dot(y_ref[...], wout_ref[...])
    o_ref[...] = x + _rms(mix, post_ref[...])


def _mixer_a(h, pre_g, post_g, w_in, ln_g, w_s, bias, w_out):
    m, d = h.shape
    inner = w_out.shape[0]
    tm = TOKEN_TILE
    tile = pl.BlockSpec((tm, d), lambda i: (i, 0))
    return pl.pallas_call(
        _mixa_kernel,
        grid=(m // tm,),
        in_specs=[tile, _row_spec(d), _row_spec(d), _resident(), _row_spec(inner),
                  _resident(), _resident(), _resident()],
        out_specs=tile,
        out_shape=jax.ShapeDtypeStruct((m, d), F32),
        scratch_shapes=[pltpu.VMEM((tm, inner), F32), pltpu.VMEM((tm, inner), BF16),
                        pltpu.VMEM((tm, inner), BF16)],
        compiler_params=_params(),
        name="mixer_a",
    )(h, pre_g, post_g, w_in, ln_g, w_s, bias, w_out)


def _qkv_kernel(h_ref, pre_ref, wqt_ref, wk_ref, wvt_ref, qt_ref, k_ref, vt_ref):
    xn = _rms(h_ref[...], pre_ref[...]).astype(BF16)
    qt_ref[...] = (_dot_nt(wqt_ref[...], xn) * (HEAD_DIM ** -0.5)).astype(BF16)
    k_ref[...] = _dot(xn, wk_ref[...]).astype(BF16)
    vt_ref[...] = _dot_nt(wvt_ref[...], xn).astype(BF16)


def _qkv(h, pre_g, wqt, wk, wvt, bsz, seq):
    m, d = h.shape
    tm = TOKEN_TILE
    nt = seq // tm
    tile = pl.BlockSpec((tm, d), lambda b, i: (b * nt + i, 0))
    tile_t = pl.BlockSpec((None, d, tm), lambda b, i: (b, 0, i))
    return pl.pallas_call(
        _qkv_kernel,
        grid=(bsz, nt),
        in_specs=[tile, _row_spec(d), _resident(), _resident(), _resident()],
        out_specs=[tile_t, tile, tile_t],
        out_shape=[jax.ShapeDtypeStruct((bsz, d, seq), BF16), jax.ShapeDtypeStruct((m, d), BF16),
                   jax.ShapeDtypeStruct((bsz, d, seq), BF16)],
        compiler_params=pltpu.CompilerParams(
            dimension_semantics=("parallel", "parallel"), vmem_limit_bytes=VMEM_LIMIT_BYTES),
        name="qkv",
    )(h, pre_g, wqt, wk, wvt)


def _attn_kernel(qt_ref, k_ref, vt_ref, o_ref, acc_ref, z_ref, a_ref):
    qi = pl.program_id(2)
    n_heads = 2 * PAIRS_PER_STEP
    row = lax.broadcasted_iota(jnp.int32, (K_BLOCK, Q_BLOCK), 0)
    col = lax.broadcasted_iota(jnp.int32, (K_BLOCK, Q_BLOCK), 1)
    tri_r = lax.broadcasted_iota(jnp.int32, (K_BLOCK, K_BLOCK), 0)
    tri_c = lax.broadcasted_iota(jnp.int32, (K_BLOCK, K_BLOCK), 1)
    upper = jnp.where(tri_r <= tri_c, 1.0, 0.0).astype(BF16)
    upper2 = jnp.concatenate([upper, upper], axis=1)
    lane = lax.broadcasted_iota(jnp.int32, (K_BLOCK, PAIR), 1)
    first_head = lane < HEAD_DIM
    items = [(b, hd) for b in range(2) for hd in range(n_heads)]

    def key_rows(n, b):
        j = 2 * (qi - n) + 1 - b
        return pl.ds(pl.multiple_of(j * K_BLOCK, K_BLOCK), K_BLOCK)

    def scores(n, slot):
        for b in range(2):
            for p in range(PAIRS_PER_STEP):
                k_pair = k_ref[key_rows(n, b), p * PAIR:(p + 1) * PAIR]
                zero = jnp.zeros_like(k_pair)
                k_split = jnp.concatenate([jnp.where(first_head, k_pair, zero),
                                           jnp.where(first_head, zero, k_pair)], axis=0)
                z_both = _dot(k_split, qt_ref[p * PAIR:(p + 1) * PAIR, :])
                for e in range(2):
                    z_ref[slot, b * n_heads + 2 * p + e] = z_both[e * K_BLOCK:(e + 1) * K_BLOCK]

    def weights(slot, carries, stricts):
        later = {}
        for i, (b, hd) in enumerate(items):
            zz = z_ref[slot, i]
            neg_abs = pltpu.bitcast(pltpu.bitcast(zz, jnp.uint32) | jnp.uint32(SIGN_BIT), F32)
            drop = jnp.maximum(zz, 0.0) + jnp.log(1.0 + jnp.exp(neg_abs))
            if stricts is not None:
                drop = jnp.where(stricts[b], drop, 0.0)
            drop_hi = drop.astype(BF16)
            drop_lo = (drop - drop_hi.astype(F32)).astype(BF16)
            later[i] = _dot(upper2, jnp.concatenate([drop_hi, drop_lo], axis=0))
        carries = list(carries)
        for i, (b, hd) in enumerate(items):
            w = jnp.exp(z_ref[slot, i] - later[i] - carries[hd])
            if stricts is not None:
                w = jnp.where(stricts[b], w, 0.0)
            a_ref[slot, i] = w.astype(BF16)
            carries[hd] = carries[hd] + later[i][0:1, :]
        return tuple(carries)

    def accumulate(n, slot):
        for hd in range(n_heads):
            hrows = slice(hd * HEAD_DIM, (hd + 1) * HEAD_DIM)
            total = acc_ref[hrows, :]
            for b in range(2):
                total = total + _dot(vt_ref[hrows, key_rows(n, b)], a_ref[slot, b * n_heads + hd])
            acc_ref[hrows, :] = total

    acc_ref[...] = jnp.zeros_like(acc_ref)
    scores(0, 0)
    scores(jnp.minimum(1, qi), 1)
    carries = tuple(jnp.zeros((1, Q_BLOCK), F32) for _ in range(n_heads))
    carries = weights(0, carries, [row + K_BLOCK < col, row < col])

    def pair(n, slot, carries):
        accumulate(n - 1, 1 - slot)
        scores(jnp.minimum(n + 1, qi), 1 - slot)
        return weights(slot, carries, None)

    def two_pairs(i, carries):
        n = 2 * i + 1
        carries = pair(n, 1, carries)
        return lax.cond(n < qi, lambda c: pair(n + 1, 0, c), lambda c: c, carries)

    lax.fori_loop(0, (qi + 1) // 2, two_pairs, carries)
    accumulate(qi, qi % 2)
    o_ref[...] = acc_ref[...].T.astype(BF16)


def _attention(qt, k, vt, bsz, seq):
    m, d = k.shape
    nq = seq // Q_BLOCK
    width = PAIRS_PER_STEP * PAIR
    n_items = 2 * 2 * PAIRS_PER_STEP
    return pl.pallas_call(
        _attn_kernel,
        grid=(bsz, d // width, nq),
        in_specs=[pl.BlockSpec((None, width, Q_BLOCK), lambda b, g, i: (b, g, i)),
                  pl.BlockSpec((seq, width), lambda b, g, i: (b, g)),
                  pl.BlockSpec((None, width, seq), lambda b, g, i: (b, g, 0))],
        out_specs=pl.BlockSpec((Q_BLOCK, width), lambda b, g, i: (b * nq + i, g)),
        out_shape=jax.ShapeDtypeStruct((m, d), BF16),
        scratch_shapes=[pltpu.VMEM((width, Q_BLOCK), F32),
                        pltpu.VMEM((2, n_items, K_BLOCK, Q_BLOCK), F32),
                        pltpu.VMEM((2, n_items, K_BLOCK, Q_BLOCK), BF16)],
        compiler_params=pltpu.CompilerParams(
            dimension_semantics=("parallel", "parallel", "arbitrary"),
            vmem_limit_bytes=VMEM_LIMIT_BYTES),
        name="stickbreak",
    )(qt, k, vt)


def _oproj_kernel(h_ref, a_ref, post_ref, wo_ref, o_ref):
    mix = _dot(a_ref[...], wo_ref[...])
    o_ref[...] = h_ref[...] + _rms(mix, post_ref[...])


def _oproj(h, attn, post_g, wo):
    m, d = h.shape
    tm = TOKEN_TILE
    tile = pl.BlockSpec((tm, d), lambda i: (i, 0))
    return pl.pallas_call(
        _oproj_kernel,
        grid=(m // tm,),
        in_specs=[tile, tile, _row_spec(d), _resident()],
        out_specs=tile,
        out_shape=jax.ShapeDtypeStruct((m, d), F32),
        compiler_params=_params(),
        name="attn_out",
    )(h, attn, post_g, wo)


def kernel(x, norm_pre, norm_post, ffn_w_gate, ffn_w_up, ffn_w_down, a_w_in, a_ln_g, a_w_s,
           a_b_s, a_w_out, b_w_qkv, b_w_o):
    bsz, seq, d = x.shape
    depth = norm_pre.shape[0]
    h = x.reshape(bsz * seq, d)
    wg, wu, wd = (w.astype(BF16) for w in (ffn_w_gate, ffn_w_up, ffn_w_down))
    a_in, a_out = a_w_in.astype(BF16), a_w_out.astype(BF16)
    b_qkv, b_o = b_w_qkv.astype(BF16), b_w_o.astype(BF16)
    gdim = a_w_out.shape[1] // A_GROUPS
    for layer in range(depth):
        pre = norm_pre[layer][:, None, :]
        post = norm_post[layer][:, None, :]
        h = _ffn(h, pre[0], post[0], wg[layer, 0], wu[layer, 0], wd[layer, 0])
        if layer % 2 == 0:
            ia = layer // 2
            bias = jnp.repeat(a_b_s[ia].T, gdim, axis=1)
            h = _mixer_a(h, pre[1], post[1], a_in[ia], a_ln_g[ia][None, :], a_w_s[ia], bias,
                         a_out[ia])
        else:
            ib = layer // 2
            w = b_qkv[ib]
            qt, k, vt = _qkv(h, pre[1], w[:, :d].T, w[:, d:2 * d], w[:, 2 * d:].T, bsz, seq)
            attn = _attention(qt, k, vt, bsz, seq)
            h = _oproj(h, attn, post[1], b_o[ib])
        h = _ffn(h, pre[2], post[2], wg[layer, 1], wu[layer, 1], wd[layer, 1])
    return h.reshape(bsz, seq, d)
```

```python
import math

import jax
import jax.numpy as jnp
from jax import lax
from jax.experimental import pallas as pl
from jax.experimental.pallas import tpu as pltpu

RMS_EPS = 1e-6
LN_EPS = 1e-5
CHUNK = 128
A_GROUPS = 8
HEAD_DIM = 64
K_BLOCK = 128
Q_BLOCK = 2 * K_BLOCK
PAIR = 2 * HEAD_DIM
PAIRS_PER_STEP = 2
SOFTPLUS_CLAMP = 80.0
FF_CHUNK = 256
TOKEN_TILE = 512
FFN_TOKEN_TILE = 1024
FFN_ROW_BLOCK = 256
VMEM_LIMIT_BYTES = 56 * 1024 * 1024

F32 = jnp.float32
BF16 = jnp.bfloat16


def _rms(x, g):
    ms = jnp.mean(x * x, axis=-1, keepdims=True)
    return x * lax.rsqrt(ms + RMS_EPS) * g


def _dot(a, b):
    return jnp.dot(a, b, preferred_element_type=F32)


def _dot_nt(a, b):
    return lax.dot_general(a, b, (((1,), (1,)), ((), ())), preferred_element_type=F32)


def _params():
    return pltpu.CompilerParams(
        dimension_semantics=("parallel",), vmem_limit_bytes=VMEM_LIMIT_BYTES)


def _resident():
    return pl.BlockSpec(memory_space=pltpu.VMEM)


def _row_spec(d):
    return pl.BlockSpec((1, d), lambda *_: (0, 0))


def _ffn_rows(x, rows, pre_ref, post_ref, wg_ref, wu_ref, wd_ref, o_ref, act_ref):
    d_ff = wg_ref.shape[1]
    xn = _rms(x, pre_ref[...]).astype(BF16)
    for c in range(d_ff // FF_CHUNK):
        sl = slice(c * FF_CHUNK, (c + 1) * FF_CHUNK)
        g = _dot(xn, wg_ref[:, sl])
        u = _dot(xn, wu_ref[:, sl])
        act_ref[rows, sl] = (g * jax.nn.sigmoid(g) * u).astype(BF16)
    f = _dot(act_ref[rows, :], wd_ref[...])
    o_ref[rows, :] = x + 0.5 * _rms(f, post_ref[...])


def _row_blocks(n_rows):
    return [slice(r, r + FFN_ROW_BLOCK) for r in range(0, n_rows, FFN_ROW_BLOCK)]


def _ffn_kernel(h_ref, pre_ref, post_ref, wg_ref, wu_ref, wd_ref, o_ref, act_ref):
    for rows in _row_blocks(h_ref.shape[0]):
        _ffn_rows(h_ref[rows, :], rows, pre_ref, post_ref, wg_ref, wu_ref, wd_ref, o_ref, act_ref)


def _attn_out_ffn_kernel(h_ref, a_ref, mix_post_ref, wo_ref, pre_ref, post_ref, wg_ref, wu_ref,
                         wd_ref, o_ref, act_ref):
    def mixed(rows):
        return h_ref[rows, :] + _rms(_dot(a_ref[rows, :], wo_ref[...]), mix_post_ref[...])

    blocks = _row_blocks(h_ref.shape[0])
    x_next = mixed(blocks[0])
    for r, rows in enumerate(blocks):
        x = x_next
        if r + 1 < len(blocks):
            x_next = mixed(blocks[r + 1])
        _ffn_rows(x, rows, pre_ref, post_ref, wg_ref, wu_ref, wd_ref, o_ref, act_ref)


def _ffn(h, pre_g, post_g, wg, wu, wd, attn_out=None):
    m, d = h.shape
    d_ff = wg.shape[1]
    tm = FFN_TOKEN_TILE
    tile = pl.BlockSpec((tm, d), lambda i: (i, 0))
    ffn_specs = [_row_spec(d), _row_spec(d), _resident(), _resident(), _resident()]
    if attn_out is None:
        body, head_specs, head_args = _ffn_kernel, [tile], (h,)
    else:
        body, head_specs = _attn_out_ffn_kernel, [tile, tile, _row_spec(d), _resident()]
        head_args = (h,) + tuple(attn_out)
    return pl.pallas_call(
        body,
        grid=(m // tm,),
        in_specs=head_specs + ffn_specs,
        out_specs=tile,
        out_shape=jax.ShapeDtypeStruct((m, d), F32),
        scratch_shapes=[pltpu.VMEM((tm, d_ff), BF16)],
        compiler_params=_params(),
        name="ffn" if attn_out is None else "attn_out_ffn",
    )(*head_args, pre_g, post_g, wg, wu, wd)


def _gelu(x):
    return 0.5 * x * (1.0 + lax.erf(x * math.sqrt(0.5)))


def _mixa_kernel(h_ref, pre_ref, post_ref, win_ref, lng_ref, ws_ref, bias_ref, wout_ref,
                 o_ref, uv_ref, u_ref, v_ref, y_ref):
    inner = wout_ref.shape[0]
    gdim = inner // A_GROUPS
    t_idx = lax.broadcasted_iota(jnp.int32, (CHUNK, CHUNK), 0)
    s_idx = lax.broadcasted_iota(jnp.int32, (CHUNK, CHUNK), 1)
    causal = s_idx <= t_idx
    w_s = [jnp.where(causal, ws_ref[g], 0.0).astype(BF16) for g in range(A_GROUPS)]
    def project_in(rows):
        xn = _rms(h_ref[rows, :], pre_ref[...]).astype(BF16)
        uv_ref[rows, :] = _dot(xn, win_ref[...])

    starts = list(range(0, h_ref.shape[0], 2 * CHUNK))
    project_in(slice(0, 2 * CHUNK))
    for r in starts:
        rows = slice(r, r + 2 * CHUNK)
        halves = [slice(r, r + CHUNK), slice(r + CHUNK, r + 2 * CHUNK)]
        if r != starts[-1]:
            project_in(slice(r + 2 * CHUNK, r + 4 * CHUNK))
        x = h_ref[rows, :]
        u_ref[rows, :] = _gelu(uv_ref[rows, :inner])
        v = _gelu(uv_ref[rows, inner:])
        mu = jnp.mean(v, axis=-1, keepdims=True)
        vc = v - mu
        var = jnp.mean(vc * vc, axis=-1, keepdims=True)
        v_ref[rows, :] = (vc * lax.rsqrt(var + LN_EPS) * lng_ref[...]).astype(BF16)
        for g in range(A_GROUPS):
            cols = slice(g * gdim, (g + 1) * gdim)
            v_pair = jnp.concatenate([v_ref[half, cols] for half in halves], axis=1)
            gated = _dot(w_s[g], v_pair)
            for c, half in enumerate(halves):
                gate = gated[:, c * gdim:(c + 1) * gdim] + bias_ref[:, cols]
                y_ref[half, cols] = (u_ref[half, cols] * gate).astype(BF16)
        mix = _dot(y_ref[rows, :], wout_ref[...])
        o_ref[rows, :] = x + _rms(mix, post_ref[...])


def _mixer_a(h, pre_g, post_g, w_in, ln_g, w_s, bias, w_out):
    m, d = h.shape
    inner = w_out.shape[0]
    tm = TOKEN_TILE
    tile = pl.BlockSpec((tm, d), lambda i: (i, 0))
    return pl.pallas_call(
        _mixa_kernel,
        grid=(m // tm,),
        in_specs=[tile, _row_spec(d), _row_spec(d), _resident(), _row_spec(inner),
                  _resident(), _resident(), _resident()],
        out_specs=tile,
        out_shape=jax.ShapeDtypeStruct((m, d), F32),
        scratch_shapes=[pltpu.VMEM((tm, 2 * inner), F32), pltpu.VMEM((tm, inner), F32),
                        pltpu.VMEM((tm, inner), BF16), pltpu.VMEM((tm, inner), BF16)],
        compiler_params=_params(),
        name="mixer_a",
    )(h, pre_g, post_g, w_in, ln_g, w_s, bias, w_out)


def _qkv_kernel(h_ref, pre_ref, wqt_ref, wk_ref, wvt_ref, qt_ref, k_ref, vt_ref):
    xn = _rms(h_ref[...], pre_ref[...]).astype(BF16)
    qt_ref[...] = (_dot_nt(wqt_ref[...], xn) * (HEAD_DIM ** -0.5)).astype(BF16)
    k_ref[...] = _dot(xn, wk_ref[...]).astype(BF16)
    vt_ref[...] = _dot_nt(wvt_ref[...], xn).astype(BF16)


def _qkv(h, pre_g, wqt, wk, wvt, bsz, seq):
    m, d = h.shape
    tm = TOKEN_TILE
    nt = seq // tm
    tile = pl.BlockSpec((tm, d), lambda b, i: (b * nt + i, 0))
    tile_t = pl.BlockSpec((None, d, tm), lambda b, i: (b, 0, i))
    return pl.pallas_call(
        _qkv_kernel,
        grid=(bsz, nt),
        in_specs=[tile, _row_spec(d), _resident(), _resident(), _resident()],
        out_specs=[tile_t, tile, tile_t],
        out_shape=[jax.ShapeDtypeStruct((bsz, d, seq), BF16), jax.ShapeDtypeStruct((m, d), BF16),
                   jax.ShapeDtypeStruct((bsz, d, seq), BF16)],
        compiler_params=pltpu.CompilerParams(
            dimension_semantics=("parallel", "parallel"), vmem_limit_bytes=VMEM_LIMIT_BYTES),
        name="qkv",
    )(h, pre_g, wqt, wk, wvt)


def _attn_kernel(qt_ref, k_ref, vt_ref, o_ref, acc_ref, z_ref, a_ref):
    n_heads = 2 * PAIRS_PER_STEP
    row = lax.broadcasted_iota(jnp.int32, (K_BLOCK, Q_BLOCK), 0)
    col = lax.broadcasted_iota(jnp.int32, (K_BLOCK, Q_BLOCK), 1)
    tri_r = lax.broadcasted_iota(jnp.int32, (K_BLOCK, K_BLOCK), 0)
    tri_c = lax.broadcasted_iota(jnp.int32, (K_BLOCK, K_BLOCK), 1)
    upper = jnp.where(tri_r <= tri_c, 1.0, 0.0).astype(BF16)
    upper2 = jnp.concatenate([upper, upper], axis=1)
    lane = lax.broadcasted_iota(jnp.int32, (K_BLOCK, PAIR), 1)
    first_head = lane < HEAD_DIM
    items = [(b, hd) for b in range(2) for hd in range(n_heads)]

    def queries(qi):
        return pl.ds(pl.multiple_of(qi * Q_BLOCK, Q_BLOCK), Q_BLOCK)

    def key_rows(qi, n, b):
        j = 2 * (qi - n) + 1 - b
        return pl.ds(pl.multiple_of(j * K_BLOCK, K_BLOCK), K_BLOCK)

    def scores(qi, n, slot):
        for b in range(2):
            for p in range(PAIRS_PER_STEP):
                k_pair = k_ref[key_rows(qi, n, b), p * PAIR:(p + 1) * PAIR]
                zero = jnp.zeros_like(k_pair)
                k_split = jnp.concatenate([jnp.where(first_head, k_pair, zero),
                                           jnp.where(first_head, zero, k_pair)], axis=0)
                z_both = _dot(k_split, qt_ref[p * PAIR:(p + 1) * PAIR, queries(qi)])
                for e in range(2):
                    z_ref[slot, b * n_heads + 2 * p + e] = z_both[e * K_BLOCK:(e + 1) * K_BLOCK]

    def weights(slot, carries, diagonal):
        def seen(b):
            return slice(K_BLOCK, Q_BLOCK) if diagonal and b == 0 else slice(0, Q_BLOCK)

        def strict(b):
            return tri_r < tri_c if b == 0 else row < col

        later = {}
        for i, (b, hd) in enumerate(items):
            zz = z_ref[slot, i, :, seen(b)]
            drop = jnp.maximum(zz, jnp.log(1.0 + jnp.exp(jnp.minimum(zz, SOFTPLUS_CLAMP))))
            if diagonal:
                drop = jnp.where(strict(b), drop, 0.0)
            drop_hi = drop.astype(BF16)
            drop_lo = (drop - drop_hi.astype(F32)).astype(BF16)
            later[i] = _dot(upper2, jnp.concatenate([drop_hi, drop_lo], axis=0))
        carries = list(carries)
        for i, (b, hd) in enumerate(items):
            w = jnp.exp(z_ref[slot, i, :, seen(b)] - later[i] - carries[hd][:, seen(b)])
            total = later[i][0:1, :]
            if diagonal:
                w = jnp.where(strict(b), w, 0.0)
                if b == 0:
                    a_ref[slot, i, :, 0:K_BLOCK] = jnp.zeros((K_BLOCK, K_BLOCK), BF16)
                    total = jnp.concatenate([jnp.zeros((1, K_BLOCK), F32), total], axis=1)
            a_ref[slot, i, :, seen(b)] = w.astype(BF16)
            carries[hd] = carries[hd] + total
        return tuple(carries)

    def accumulate(qi, n, slot):
        for hd in range(n_heads):
            hrows = slice(hd * HEAD_DIM, (hd + 1) * HEAD_DIM)
            total = acc_ref[hrows, :]
            for b in range(2):
                total = total + _dot(vt_ref[hrows, key_rows(qi, n, b)], a_ref[slot, b * n_heads + hd])
            acc_ref[hrows, :] = total

    def query_block(qi, _):
        acc_ref[...] = jnp.zeros_like(acc_ref)
        scores(qi, 0, 0)
        scores(qi, jnp.minimum(1, qi), 1)
        carries = tuple(jnp.zeros((1, Q_BLOCK), F32) for _ in range(n_heads))
        carries = weights(0, carries, True)

        def pair(n, slot, carries):
            accumulate(qi, n - 1, 1 - slot)
            scores(qi, jnp.minimum(n + 1, qi), 1 - slot)
            return weights(slot, carries, False)

        def two_pairs(i, carries):
            n = 2 * i + 1
            carries = pair(n, 1, carries)
            return lax.cond(n < qi, lambda c: pair(n + 1, 0, c), lambda c: c, carries)

        lax.fori_loop(0, (qi + 1) // 2, two_pairs, carries)
        accumulate(qi, qi, qi % 2)
        o_ref[queries(qi), :] = acc_ref[...].T.astype(BF16)
        return 0

    lax.fori_loop(0, o_ref.shape[0] // Q_BLOCK, query_block, 0)


def _attention(qt, k, vt, bsz, seq):
    m, d = k.shape
    width = PAIRS_PER_STEP * PAIR
    n_items = 2 * 2 * PAIRS_PER_STEP
    return pl.pallas_call(
        _attn_kernel,
        grid=(bsz, d // width),
        in_specs=[pl.BlockSpec((None, width, seq), lambda b, g: (b, g, 0)),
                  pl.BlockSpec((seq, width), lambda b, g: (b, g)),
                  pl.BlockSpec((None, width, seq), lambda b, g: (b, g, 0))],
        out_specs=pl.BlockSpec((seq, width), lambda b, g: (b, g)),
        out_shape=jax.ShapeDtypeStruct((m, d), BF16),
        scratch_shapes=[pltpu.VMEM((width, Q_BLOCK), F32),
                        pltpu.VMEM((2, n_items, K_BLOCK, Q_BLOCK), F32),
                        pltpu.VMEM((2, n_items, K_BLOCK, Q_BLOCK), BF16)],
        compiler_params=pltpu.CompilerParams(
            dimension_semantics=("parallel", "parallel"),
            vmem_limit_bytes=VMEM_LIMIT_BYTES),
        name="stickbreak",
    )(qt, k, vt)


def kernel(x, norm_pre, norm_post, ffn_w_gate, ffn_w_up, ffn_w_down, a_w_in, a_ln_g, a_w_s,
           a_b_s, a_w_out, b_w_qkv, b_w_o):
    bsz, seq, d = x.shape
    depth = norm_pre.shape[0]
    h = x.reshape(bsz * seq, d)
    wg, wu, wd = (w.astype(BF16) for w in (ffn_w_gate, ffn_w_up, ffn_w_down))
    a_in, a_out = a_w_in.astype(BF16), a_w_out.astype(BF16)
    b_qkv, b_o = b_w_qkv.astype(BF16), b_w_o.astype(BF16)
    gdim = a_w_out.shape[1] // A_GROUPS
    for layer in range(depth):
        pre = norm_pre[layer][:, None, :]
        post = norm_post[layer][:, None, :]
        h = _ffn(h, pre[0], post[0], wg[layer, 0], wu[layer, 0], wd[layer, 0])
        attn_out = None
        if layer % 2 == 0:
            ia = layer // 2
            bias = jnp.repeat(a_b_s[ia].T, gdim, axis=1)
            h = _mixer_a(h, pre[1], post[1], a_in[ia], a_ln_g[ia][None, :], a_w_s[ia], bias,
                         a_out[ia])
        else:
            ib = layer // 2
            w = b_qkv[ib]
            qt, k, vt = _qkv(h, pre[1], w[:, :d].T, w[:, d:2 * d], w[:, 2 * d:].T, bsz, seq)
            attn_out = (_attention(qt, k, vt, bsz, seq), post[1], b_o[ib])
        h = _ffn(h, pre[2], post[2], wg[layer, 1], wu[layer, 1], wd[layer, 1], attn_out)
    return h.reshape(bsz, seq, d)
```

```python
import math

import jax
import jax.numpy as jnp
from jax import lax
from jax.experimental import pallas as pl
from jax.experimental.pallas import tpu as pltpu

RMS_EPS = 1e-6
LN_EPS = 1e-5
CHUNK = 128
A_GROUPS = 8
HEAD_DIM = 64

LANES = 128
MXU_DIM = 256
V7X_VMEM_BYTES = 64 * 1024 * 1024
VMEM_LIMIT_BYTES = V7X_VMEM_BYTES * 7 // 8

PAIR = 2 * HEAD_DIM
PAIRS_PER_STEP = 2
K_BLOCK = LANES
Q_BLOCK = MXU_DIM
FF_CHUNK = MXU_DIM
TOKEN_TILE = 512
FFN_TOKEN_TILE = 1024
FFN_ROW_BLOCK = MXU_DIM

SOFTPLUS_CLAMP = 80.0

F32 = jnp.float32
BF16 = jnp.bfloat16


def _rms(x, g):
    ms = jnp.mean(x * x, axis=-1, keepdims=True)
    return x * lax.rsqrt(ms + RMS_EPS) * g


def _dot(a, b):
    return jnp.dot(a, b, preferred_element_type=F32)


def _dot_nt(a, b):
    return lax.dot_general(a, b, (((1,), (1,)), ((), ())), preferred_element_type=F32)


def _params(grid_rank=1):
    return pltpu.CompilerParams(
        dimension_semantics=("parallel",) * grid_rank, vmem_limit_bytes=VMEM_LIMIT_BYTES)


def _resident():
    return pl.BlockSpec(memory_space=pltpu.VMEM)


def _row_spec(d):
    return pl.BlockSpec((1, d), lambda *_: (0, 0))


def _ffn_rows(x, rows, pre_ref, post_ref, wg_ref, wu_ref, wd_ref, o_ref, act_ref):
    d_ff = wg_ref.shape[1]
    xn = _rms(x, pre_ref[...]).astype(BF16)
    for c in range(d_ff // FF_CHUNK):
        sl = slice(c * FF_CHUNK, (c + 1) * FF_CHUNK)
        g = _dot(xn, wg_ref[:, sl])
        u = _dot(xn, wu_ref[:, sl])
        act_ref[rows, sl] = (g * jax.nn.sigmoid(g) * u).astype(BF16)
    f = _dot(act_ref[rows, :], wd_ref[...])
    o_ref[rows, :] = x + 0.5 * _rms(f, post_ref[...])


def _row_blocks(n_rows):
    return [slice(r, r + FFN_ROW_BLOCK) for r in range(0, n_rows, FFN_ROW_BLOCK)]


def _ffn_kernel(h_ref, pre_ref, post_ref, wg_ref, wu_ref, wd_ref, o_ref, act_ref):
    for rows in _row_blocks(h_ref.shape[0]):
        _ffn_rows(h_ref[rows, :], rows, pre_ref, post_ref, wg_ref, wu_ref, wd_ref, o_ref, act_ref)


def _attn_out_ffn_kernel(h_ref, a_ref, mix_post_ref, wo_ref, pre_ref, post_ref, wg_ref, wu_ref,
                         wd_ref, o_ref, act_ref):
    def mixed(rows):
        return h_ref[rows, :] + _rms(_dot(a_ref[rows, :], wo_ref[...]), mix_post_ref[...])

    blocks = _row_blocks(h_ref.shape[0])
    x_next = mixed(blocks[0])
    for r, rows in enumerate(blocks):
        x = x_next
        if r + 1 < len(blocks):
            x_next = mixed(blocks[r + 1])
        _ffn_rows(x, rows, pre_ref, post_ref, wg_ref, wu_ref, wd_ref, o_ref, act_ref)


def _ffn(h, pre_g, post_g, wg, wu, wd, attn_out=None):
    m, d = h.shape
    d_ff = wg.shape[1]
    tm = FFN_TOKEN_TILE
    tile = pl.BlockSpec((tm, d), lambda i: (i, 0))
    ffn_specs = [_row_spec(d), _row_spec(d), _resident(), _resident(), _resident()]
    if attn_out is None:
        body, head_specs, head_args = _ffn_kernel, [tile], (h,)
    else:
        body, head_specs = _attn_out_ffn_kernel, [tile, tile, _row_spec(d), _resident()]
        head_args = (h,) + tuple(attn_out)
    return pl.pallas_call(
        body,
        grid=(m // tm,),
        in_specs=head_specs + ffn_specs,
        out_specs=tile,
        out_shape=jax.ShapeDtypeStruct((m, d), F32),
        scratch_shapes=[pltpu.VMEM((tm, d_ff), BF16)],
        compiler_params=_params(),
        name="ffn" if attn_out is None else "attn_out_ffn",
    )(*head_args, pre_g, post_g, wg, wu, wd)


def _gelu(x):
    return 0.5 * x * (1.0 + lax.erf(x * math.sqrt(0.5)))


def _mixa_kernel(h_ref, pre_ref, post_ref, win_ref, lng_ref, ws_ref, bias_ref, wout_ref,
                 o_ref, uv_ref, u_ref, v_ref, y_ref):
    inner = wout_ref.shape[0]
    gdim = inner // A_GROUPS
    t_idx = lax.broadcasted_iota(jnp.int32, (CHUNK, CHUNK), 0)
    s_idx = lax.broadcasted_iota(jnp.int32, (CHUNK, CHUNK), 1)
    causal = s_idx <= t_idx
    w_s = [jnp.where(causal, ws_ref[g], 0.0).astype(BF16) for g in range(A_GROUPS)]
    def project_in(rows):
        xn = _rms(h_ref[rows, :], pre_ref[...]).astype(BF16)
        uv_ref[rows, :] = _dot(xn, win_ref[...])

    starts = list(range(0, h_ref.shape[0], 2 * CHUNK))
    project_in(slice(0, 2 * CHUNK))
    for r in starts:
        rows = slice(r, r + 2 * CHUNK)
        halves = [slice(r, r + CHUNK), slice(r + CHUNK, r + 2 * CHUNK)]
        if r != starts[-1]:
            project_in(slice(r + 2 * CHUNK, r + 4 * CHUNK))
        x = h_ref[rows, :]
        u_ref[rows, :] = _gelu(uv_ref[rows, :inner])
        v = _gelu(uv_ref[rows, inner:])
        mu = jnp.mean(v, axis=-1, keepdims=True)
        vc = v - mu
        var = jnp.mean(vc * vc, axis=-1, keepdims=True)
        v_ref[rows, :] = (vc * lax.rsqrt(var + LN_EPS) * lng_ref[...]).astype(BF16)
        for g in range(A_GROUPS):
            cols = slice(g * gdim, (g + 1) * gdim)
            v_pair = jnp.concatenate([v_ref[half, cols] for half in halves], axis=1)
            gated = _dot(w_s[g], v_pair)
            for c, half in enumerate(halves):
                gate = gated[:, c * gdim:(c + 1) * gdim] + bias_ref[:, cols]
                y_ref[half, cols] = (u_ref[half, cols] * gate).astype(BF16)
        mix = _dot(y_ref[rows, :], wout_ref[...])
        o_ref[rows, :] = x + _rms(mix, post_ref[...])


def _mixer_a(h, pre_g, post_g, w_in, ln_g, w_s, bias, w_out):
    m, d = h.shape
    inner = w_out.shape[0]
    tm = TOKEN_TILE
    tile = pl.BlockSpec((tm, d), lambda i: (i, 0))
    return pl.pallas_call(
        _mixa_kernel,
        grid=(m // tm,),
        in_specs=[tile, _row_spec(d), _row_spec(d), _resident(), _row_spec(inner),
                  _resident(), _resident(), _resident()],
        out_specs=tile,
        out_shape=jax.ShapeDtypeStruct((m, d), F32),
        scratch_shapes=[pltpu.VMEM((tm, 2 * inner), F32), pltpu.VMEM((tm, inner), F32),
                        pltpu.VMEM((tm, inner), BF16), pltpu.VMEM((tm, inner), BF16)],
        compiler_params=_params(),
        name="mixer_a",
    )(h, pre_g, post_g, w_in, ln_g, w_s, bias, w_out)


def _qkv_kernel(h_ref, pre_ref, wqt_ref, wk_ref, wvt_ref, qt_ref, k_ref, vt_ref):
    xn = _rms(h_ref[...], pre_ref[...]).astype(BF16)
    qt_ref[...] = (_dot_nt(wqt_ref[...], xn) * (HEAD_DIM ** -0.5)).astype(BF16)
    k_ref[...] = _dot(xn, wk_ref[...]).astype(BF16)
    vt_ref[...] = _dot_nt(wvt_ref[...], xn).astype(BF16)


def _qkv(h, pre_g, wqt, wk, wvt, bsz, seq):
    m, d = h.shape
    tm = TOKEN_TILE
    nt = seq // tm
    tile = pl.BlockSpec((tm, d), lambda b, i: (b * nt + i, 0))
    tile_t = pl.BlockSpec((None, d, tm), lambda b, i: (b, 0, i))
    return pl.pallas_call(
        _qkv_kernel,
        grid=(bsz, nt),
        in_specs=[tile, _row_spec(d), _resident(), _resident(), _resident()],
        out_specs=[tile_t, tile, tile_t],
        out_shape=[jax.ShapeDtypeStruct((bsz, d, seq), BF16), jax.ShapeDtypeStruct((m, d), BF16),
                   jax.ShapeDtypeStruct((bsz, d, seq), BF16)],
        compiler_params=_params(2),
        name="qkv",
    )(h, pre_g, wqt, wk, wvt)


def _attn_kernel(qt_ref, k_ref, vt_ref, o_ref, acc_ref, z_ref, a_ref):
    n_heads = 2 * PAIRS_PER_STEP
    row = lax.broadcasted_iota(jnp.int32, (K_BLOCK, Q_BLOCK), 0)
    col = lax.broadcasted_iota(jnp.int32, (K_BLOCK, Q_BLOCK), 1)
    tri_r = lax.broadcasted_iota(jnp.int32, (K_BLOCK, K_BLOCK), 0)
    tri_c = lax.broadcasted_iota(jnp.int32, (K_BLOCK, K_BLOCK), 1)
    upper = jnp.where(tri_r <= tri_c, 1.0, 0.0).astype(BF16)
    upper2 = jnp.concatenate([upper, upper], axis=1)
    lane = lax.broadcasted_iota(jnp.int32, (K_BLOCK, PAIR), 1)
    first_head = lane < HEAD_DIM
    items = [(b, hd) for b in range(2) for hd in range(n_heads)]

    def queries(qi):
        return pl.ds(pl.multiple_of(qi * Q_BLOCK, Q_BLOCK), Q_BLOCK)

    def key_rows(qi, n, b):
        j = 2 * (qi - n) + 1 - b
        return pl.ds(pl.multiple_of(j * K_BLOCK, K_BLOCK), K_BLOCK)

    def scores(qi, n, slot):
        for b in range(2):
            for p in range(PAIRS_PER_STEP):
                k_pair = k_ref[key_rows(qi, n, b), p * PAIR:(p + 1) * PAIR]
                zero = jnp.zeros_like(k_pair)
                k_split = jnp.concatenate([jnp.where(first_head, k_pair, zero),
                                           jnp.where(first_head, zero, k_pair)], axis=0)
                z_both = _dot(k_split, qt_ref[p * PAIR:(p + 1) * PAIR, queries(qi)])
                for e in range(2):
                    z_ref[slot, b * n_heads + 2 * p + e] = z_both[e * K_BLOCK:(e + 1) * K_BLOCK]

    def weights(slot, carries, diagonal):
        def seen(b):
            return slice(K_BLOCK, Q_BLOCK) if diagonal and b == 0 else slice(0, Q_BLOCK)

        def strict(b):
            return tri_r < tri_c if b == 0 else row < col

        later = {}
        for i, (b, hd) in enumerate(items):
            zz = z_ref[slot, i, :, seen(b)]
            drop = jnp.maximum(zz, jnp.log(1.0 + jnp.exp(jnp.minimum(zz, SOFTPLUS_CLAMP))))
            if diagonal:
                drop = jnp.where(strict(b), drop, 0.0)
            drop_hi = drop.astype(BF16)
            drop_lo = (drop - drop_hi.astype(F32)).astype(BF16)
            later[i] = _dot(upper2, jnp.concatenate([drop_hi, drop_lo], axis=0))
        carries = list(carries)
        for i, (b, hd) in enumerate(items):
            w = jnp.exp(z_ref[slot, i, :, seen(b)] - later[i] - carries[hd][:, seen(b)])
            total = later[i][0:1, :]
            if diagonal:
                w = jnp.where(strict(b), w, 0.0)
                if b == 0:
                    a_ref[slot, i, :, 0:K_BLOCK] = jnp.zeros((K_BLOCK, K_BLOCK), BF16)
                    total = jnp.concatenate([jnp.zeros((1, K_BLOCK), F32), total], axis=1)
            a_ref[slot, i, :, seen(b)] = w.astype(BF16)
            carries[hd] = carries[hd] + total
        return tuple(carries)

    def accumulate(qi, n, slot):
        for hd in range(n_heads):
            hrows = slice(hd * HEAD_DIM, (hd + 1) * HEAD_DIM)
            total = acc_ref[hrows, :]
            for b in range(2):
                total = total + _dot(vt_ref[hrows, key_rows(qi, n, b)], a_ref[slot, b * n_heads + hd])
            acc_ref[hrows, :] = total

    n_blocks = o_ref.shape[0] // Q_BLOCK

    def query_block(qi, _):
        acc_ref[...] = jnp.zeros_like(acc_ref)
        scores(qi, jnp.minimum(1, qi), 1)
        carries = tuple(jnp.zeros((1, Q_BLOCK), F32) for _ in range(n_heads))
        carries = weights(0, carries, True)

        def pair(n, slot, carries):
            accumulate(qi, n - 1, 1 - slot)
            scores(qi, jnp.minimum(n + 1, qi), 1 - slot)
            return weights(slot, carries, False)

        def two_pairs(i, carries):
            n = 2 * i + 1
            carries = pair(n, 1, carries)
            return lax.cond(n < qi, lambda c: pair(n + 1, 0, c), lambda c: c, carries)

        lax.fori_loop(0, (qi + 1) // 2, two_pairs, carries)
        accumulate(qi, qi, qi % 2)
        scores(jnp.minimum(qi + 1, n_blocks - 1), 0, 0)
        o_ref[queries(qi), :] = acc_ref[...].T.astype(BF16)
        return 0

    scores(0, 0, 0)
    lax.fori_loop(0, n_blocks, query_block, 0)


def _attention(qt, k, vt, bsz, seq):
    m, d = k.shape
    width = PAIRS_PER_STEP * PAIR
    n_items = 2 * 2 * PAIRS_PER_STEP
    return pl.pallas_call(
        _attn_kernel,
        grid=(bsz, d // width),
        in_specs=[pl.BlockSpec((None, width, seq), lambda b, g: (b, g, 0)),
                  pl.BlockSpec((seq, width), lambda b, g: (b, g)),
                  pl.BlockSpec((None, width, seq), lambda b, g: (b, g, 0))],
        out_specs=pl.BlockSpec((seq, width), lambda b, g: (b, g)),
        out_shape=jax.ShapeDtypeStruct((m, d), BF16),
        scratch_shapes=[pltpu.VMEM((width, Q_BLOCK), F32),
                        pltpu.VMEM((2, n_items, K_BLOCK, Q_BLOCK), F32),
                        pltpu.VMEM((2, n_items, K_BLOCK, Q_BLOCK), BF16)],
        compiler_params=_params(2),
        name="stickbreak",
    )(qt, k, vt)


def kernel(x, norm_pre, norm_post, ffn_w_gate, ffn_w_up, ffn_w_down, a_w_in, a_ln_g, a_w_s,
           a_b_s, a_w_out, b_w_qkv, b_w_o):
    bsz, seq, d = x.shape
    depth = norm_pre.shape[0]
    assert seq % TOKEN_TILE == 0 and (bsz * seq) % FFN_TOKEN_TILE == 0, (bsz, seq)
    assert TOKEN_TILE % (2 * CHUNK) == 0 and TOKEN_TILE % Q_BLOCK == 0
    assert d % (PAIRS_PER_STEP * PAIR) == 0 and ffn_w_gate.shape[-1] % FF_CHUNK == 0
    assert a_w_out.shape[1] == A_GROUPS * LANES and a_w_s.shape[-1] == CHUNK
    h = x.reshape(bsz * seq, d)
    wg, wu, wd = (w.astype(BF16) for w in (ffn_w_gate, ffn_w_up, ffn_w_down))
    a_in, a_out = a_w_in.astype(BF16), a_w_out.astype(BF16)
    b_qkv, b_o = b_w_qkv.astype(BF16), b_w_o.astype(BF16)
    gdim = a_w_out.shape[1] // A_GROUPS
    for layer in range(depth):
        pre = norm_pre[layer][:, None, :]
        post = norm_post[layer][:, None, :]
        h = _ffn(h, pre[0], post[0], wg[layer, 0], wu[layer, 0], wd[layer, 0])
        attn_out = None
        if layer % 2 == 0:
            ia = layer // 2
            bias = jnp.repeat(a_b_s[ia].T, gdim, axis=1)
            h = _mixer_a(h, pre[1], post[1], a_in[ia], a_ln_g[ia][None, :], a_w_s[ia], bias,
                         a_out[ia])
        else:
            ib = layer // 2
            w = b_qkv[ib]
            qt, k, vt = _qkv(h, pre[1], w[:, :d].T, w[:, d:2 * d], w[:, 2 * d:].T, bsz, seq)
            attn_out = (_attention(qt, k, vt, bsz, seq), post[1], b_o[ib])
        h = _ffn(h, pre[2], post[2], wg[layer, 1], wu[layer, 1], wd[layer, 1], attn_out)
    return h.reshape(bsz, seq, d)
```

```python
import math

import jax
import jax.numpy as jnp
from jax import lax
from jax.experimental import pallas as pl
from jax.experimental.pallas import tpu as pltpu

RMS_EPS = 1e-6
LN_EPS = 1e-5
CHUNK = 128
A_GROUPS = 8
HEAD_DIM = 64

LANES = 128
MXU_DIM = 256
V7X_VMEM_BYTES = 64 * 1024 * 1024
VMEM_LIMIT_BYTES = V7X_VMEM_BYTES * 7 // 8

PAIR = 2 * HEAD_DIM
PAIRS_PER_STEP = 2
K_BLOCK = LANES
Q_BLOCK = MXU_DIM
FF_CHUNK = MXU_DIM
TOKEN_TILE = 1024
FFN_TOKEN_TILE = 1024
FFN_ROW_BLOCK = MXU_DIM

SOFTPLUS_CLAMP = 80.0

F32 = jnp.float32
BF16 = jnp.bfloat16


def _rms(x, g):
    ms = jnp.mean(x * x, axis=-1, keepdims=True)
    return x * lax.rsqrt(ms + RMS_EPS) * g


def _dot(a, b):
    return jnp.dot(a, b, preferred_element_type=F32)


def _dot_nt(a, b):
    return lax.dot_general(a, b, (((1,), (1,)), ((), ())), preferred_element_type=F32)


def _params(grid_rank=1):
    return pltpu.CompilerParams(
        dimension_semantics=("parallel",) * grid_rank, vmem_limit_bytes=VMEM_LIMIT_BYTES)


def _resident():
    return pl.BlockSpec(memory_space=pltpu.VMEM)


def _row_spec(d):
    return pl.BlockSpec((1, d), lambda *_: (0, 0))


def _ffn_rows(x, rows, pre_ref, post_ref, wg_ref, wu_ref, wd_ref, o_ref, act_ref):
    d_ff = wg_ref.shape[1]
    xn = _rms(x, pre_ref[...]).astype(BF16)
    for c in range(d_ff // FF_CHUNK):
        sl = slice(c * FF_CHUNK, (c + 1) * FF_CHUNK)
        g = _dot(xn, wg_ref[:, sl])
        u = _dot(xn, wu_ref[:, sl])
        act_ref[rows, sl] = (g * jax.nn.sigmoid(g) * u).astype(BF16)
    f = _dot(act_ref[rows, :], wd_ref[...])
    o_ref[rows, :] = x + 0.5 * _rms(f, post_ref[...])


def _row_blocks(n_rows):
    return [slice(r, r + FFN_ROW_BLOCK) for r in range(0, n_rows, FFN_ROW_BLOCK)]


def _ffn_kernel(h_ref, pre_ref, post_ref, wg_ref, wu_ref, wd_ref, o_ref, act_ref):
    for rows in _row_blocks(h_ref.shape[0]):
        _ffn_rows(h_ref[rows, :], rows, pre_ref, post_ref, wg_ref, wu_ref, wd_ref, o_ref, act_ref)


def _attn_out_ffn_kernel(h_ref, a_ref, mix_post_ref, wo_ref, pre_ref, post_ref, wg_ref, wu_ref,
                         wd_ref, o_ref, act_ref):
    def mixed(rows):
        return h_ref[rows, :] + _rms(_dot(a_ref[rows, :], wo_ref[...]), mix_post_ref[...])

    blocks = _row_blocks(h_ref.shape[0])
    x_next = mixed(blocks[0])
    for r, rows in enumerate(blocks):
        x = x_next
        if r + 1 < len(blocks):
            x_next = mixed(blocks[r + 1])
        _ffn_rows(x, rows, pre_ref, post_ref, wg_ref, wu_ref, wd_ref, o_ref, act_ref)


def _ffn(h, pre_g, post_g, wg, wu, wd, attn_out=None):
    m, d = h.shape
    d_ff = wg.shape[1]
    tm = FFN_TOKEN_TILE
    tile = pl.BlockSpec((tm, d), lambda i: (i, 0))
    ffn_specs = [_row_spec(d), _row_spec(d), _resident(), _resident(), _resident()]
    if attn_out is None:
        body, head_specs, head_args = _ffn_kernel, [tile], (h,)
    else:
        body, head_specs = _attn_out_ffn_kernel, [tile, tile, _row_spec(d), _resident()]
        head_args = (h,) + tuple(attn_out)
    return pl.pallas_call(
        body,
        grid=(m // tm,),
        in_specs=head_specs + ffn_specs,
        out_specs=tile,
        out_shape=jax.ShapeDtypeStruct((m, d), F32),
        scratch_shapes=[pltpu.VMEM((tm, d_ff), BF16)],
        compiler_params=_params(),
        name="ffn" if attn_out is None else "attn_out_ffn",
    )(*head_args, pre_g, post_g, wg, wu, wd)


def _gelu(x):
    return 0.5 * x * (1.0 + lax.erf(x * math.sqrt(0.5)))


def _mixa_kernel(h_ref, pre_ref, post_ref, win_ref, lng_ref, ws_ref, bias_ref, wout_ref,
                 o_ref, uv_ref, u_ref, v_ref, y_ref):
    inner = wout_ref.shape[0]
    gdim = inner // A_GROUPS
    t_idx = lax.broadcasted_iota(jnp.int32, (CHUNK, CHUNK), 0)
    s_idx = lax.broadcasted_iota(jnp.int32, (CHUNK, CHUNK), 1)
    causal = s_idx <= t_idx
    w_s = [jnp.where(causal, ws_ref[g], 0.0).astype(BF16) for g in range(A_GROUPS)]
    def project_in(rows):
        xn = _rms(h_ref[rows, :], pre_ref[...]).astype(BF16)
        uv_ref[rows, :] = _dot(xn, win_ref[...])

    starts = list(range(0, h_ref.shape[0], 2 * CHUNK))
    project_in(slice(0, 2 * CHUNK))
    for r in starts:
        rows = slice(r, r + 2 * CHUNK)
        halves = [slice(r, r + CHUNK), slice(r + CHUNK, r + 2 * CHUNK)]
        if r != starts[-1]:
            project_in(slice(r + 2 * CHUNK, r + 4 * CHUNK))
        x = h_ref[rows, :]
        u_ref[rows, :] = _gelu(uv_ref[rows, :inner])
        v = _gelu(uv_ref[rows, inner:])
        mu = jnp.mean(v, axis=-1, keepdims=True)
        vc = v - mu
        var = jnp.mean(vc * vc, axis=-1, keepdims=True)
        v_ref[rows, :] = (vc * lax.rsqrt(var + LN_EPS) * lng_ref[...]).astype(BF16)
        for g in range(A_GROUPS):
            cols = slice(g * gdim, (g + 1) * gdim)
            v_pair = jnp.concatenate([v_ref[half, cols] for half in halves], axis=1)
            gated = _dot(w_s[g], v_pair)
            for c, half in enumerate(halves):
                gate = gated[:, c * gdim:(c + 1) * gdim] + bias_ref[:, cols]
                y_ref[half, cols] = (u_ref[half, cols] * gate).astype(BF16)
        mix = _dot(y_ref[rows, :], wout_ref[...])
        o_ref[rows, :] = x + _rms(mix, post_ref[...])


def _mixer_a(h, pre_g, post_g, w_in, ln_g, w_s, bias, w_out):
    m, d = h.shape
    inner = w_out.shape[0]
    tm = TOKEN_TILE
    tile = pl.BlockSpec((tm, d), lambda i: (i, 0))
    return pl.pallas_call(
        _mixa_kernel,
        grid=(m // tm,),
        in_specs=[tile, _row_spec(d), _row_spec(d), _resident(), _row_spec(inner),
                  _resident(), _resident(), _resident()],
        out_specs=tile,
        out_shape=jax.ShapeDtypeStruct((m, d), F32),
        scratch_shapes=[pltpu.VMEM((tm, 2 * inner), F32), pltpu.VMEM((tm, inner), F32),
                        pltpu.VMEM((tm, inner), BF16), pltpu.VMEM((tm, inner), BF16)],
        compiler_params=_params(),
        name="mixer_a",
    )(h, pre_g, post_g, w_in, ln_g, w_s, bias, w_out)


def _qkv_kernel(h_ref, pre_ref, wqt_ref, wk_ref, wvt_ref, qt_ref, k_ref, vt_ref):
    for rows in _row_blocks(h_ref.shape[0]):
        xn = _rms(h_ref[rows, :], pre_ref[...]).astype(BF16)
        qt_ref[:, rows] = (_dot_nt(wqt_ref[...], xn) * (HEAD_DIM ** -0.5)).astype(BF16)
        k_ref[rows, :] = _dot(xn, wk_ref[...]).astype(BF16)
        vt_ref[:, rows] = _dot_nt(wvt_ref[...], xn).astype(BF16)


def _qkv(h, pre_g, wqt, wk, wvt, bsz, seq):
    m, d = h.shape
    tm = TOKEN_TILE
    nt = seq // tm
    tile = pl.BlockSpec((tm, d), lambda b, i: (b * nt + i, 0))
    tile_t = pl.BlockSpec((None, d, tm), lambda b, i: (b, 0, i))
    return pl.pallas_call(
        _qkv_kernel,
        grid=(bsz, nt),
        in_specs=[tile, _row_spec(d), _resident(), _resident(), _resident()],
        out_specs=[tile_t, tile, tile_t],
        out_shape=[jax.ShapeDtypeStruct((bsz, d, seq), BF16), jax.ShapeDtypeStruct((m, d), BF16),
                   jax.ShapeDtypeStruct((bsz, d, seq), BF16)],
        compiler_params=_params(2),
        name="qkv",
    )(h, pre_g, wqt, wk, wvt)


def _attn_kernel(qt_ref, k_ref, vt_ref, o_ref, acc_ref, z_ref, a_ref):
    n_heads = 2 * PAIRS_PER_STEP
    row = lax.broadcasted_iota(jnp.int32, (K_BLOCK, Q_BLOCK), 0)
    col = lax.broadcasted_iota(jnp.int32, (K_BLOCK, Q_BLOCK), 1)
    tri_r = lax.broadcasted_iota(jnp.int32, (K_BLOCK, K_BLOCK), 0)
    tri_c = lax.broadcasted_iota(jnp.int32, (K_BLOCK, K_BLOCK), 1)
    upper = jnp.where(tri_r <= tri_c, 1.0, 0.0).astype(BF16)
    upper2 = jnp.concatenate([upper, upper], axis=1)
    lane = lax.broadcasted_iota(jnp.int32, (K_BLOCK, PAIR), 1)
    first_head = lane < HEAD_DIM
    items = [(b, hd) for b in range(2) for hd in range(n_heads)]

    def queries(qi):
        return pl.ds(pl.multiple_of(qi * Q_BLOCK, Q_BLOCK), Q_BLOCK)

    def key_rows(qi, n, b):
        j = 2 * (qi - n) + 1 - b
        return pl.ds(pl.multiple_of(j * K_BLOCK, K_BLOCK), K_BLOCK)

    def scores(qi, n, slot):
        for b in range(2):
            for p in range(PAIRS_PER_STEP):
                k_pair = k_ref[key_rows(qi, n, b), p * PAIR:(p + 1) * PAIR]
                zero = jnp.zeros_like(k_pair)
                k_split = jnp.concatenate([jnp.where(first_head, k_pair, zero),
                                           jnp.where(first_head, zero, k_pair)], axis=0)
                z_both = _dot(k_split, qt_ref[p * PAIR:(p + 1) * PAIR, queries(qi)])
                for e in range(2):
                    z_ref[slot, b * n_heads + 2 * p + e] = z_both[e * K_BLOCK:(e + 1) * K_BLOCK]

    def weights(slot, carries, diagonal):
        def seen(b):
            return slice(K_BLOCK, Q_BLOCK) if diagonal and b == 0 else slice(0, Q_BLOCK)

        def strict(b):
            return tri_r < tri_c if b == 0 else row < col

        later = {}
        for i, (b, hd) in enumerate(items):
            zz = z_ref[slot, i, :, seen(b)]
            drop = jnp.maximum(zz, jnp.log(1.0 + jnp.exp(jnp.minimum(zz, SOFTPLUS_CLAMP))))
            if diagonal:
                drop = jnp.where(strict(b), drop, 0.0)
            drop_hi = drop.astype(BF16)
            drop_lo = (drop - drop_hi.astype(F32)).astype(BF16)
            later[i] = _dot(upper2, jnp.concatenate([drop_hi, drop_lo], axis=0))
        carries = list(carries)
        for i, (b, hd) in enumerate(items):
            w = jnp.exp(z_ref[slot, i, :, seen(b)] - later[i] - carries[hd][:, seen(b)])
            total = later[i][0:1, :]
            if diagonal:
                w = jnp.where(strict(b), w, 0.0)
                if b == 0:
                    a_ref[slot, i, :, 0:K_BLOCK] = jnp.zeros((K_BLOCK, K_BLOCK), BF16)
                    total = jnp.concatenate([jnp.zeros((1, K_BLOCK), F32), total], axis=1)
            a_ref[slot, i, :, seen(b)] = w.astype(BF16)
            carries[hd] = carries[hd] + total
        return tuple(carries)

    def accumulate(qi, n, slot):
        for hd in range(n_heads):
            hrows = slice(hd * HEAD_DIM, (hd + 1) * HEAD_DIM)
            total = acc_ref[hrows, :]
            for b in range(2):
                total = total + _dot(vt_ref[hrows, key_rows(qi, n, b)], a_ref[slot, b * n_heads + hd])
            acc_ref[hrows, :] = total

    n_blocks = o_ref.shape[0] // Q_BLOCK

    def query_block(qi, _):
        acc_ref[...] = jnp.zeros_like(acc_ref)
        scores(qi, jnp.minimum(1, qi), 1)
        carries = tuple(jnp.zeros((1, Q_BLOCK), F32) for _ in range(n_heads))
        carries = weights(0, carries, True)

        def pair(n, slot, carries):
            accumulate(qi, n - 1, 1 - slot)
            scores(qi, jnp.minimum(n + 1, qi), 1 - slot)
            return weights(slot, carries, False)

        def two_pairs(i, carries):
            n = 2 * i + 1
            carries = pair(n, 1, carries)
            return lax.cond(n < qi, lambda c: pair(n + 1, 0, c), lambda c: c, carries)

        lax.fori_loop(0, (qi + 1) // 2, two_pairs, carries)
        accumulate(qi, qi, qi % 2)
        scores(jnp.minimum(qi + 1, n_blocks - 1), 0, 0)
        o_ref[queries(qi), :] = acc_ref[...].T.astype(BF16)
        return 0

    scores(0, 0, 0)
    lax.fori_loop(0, n_blocks, query_block, 0)


def _attention(qt, k, vt, bsz, seq):
    m, d = k.shape
    width = PAIRS_PER_STEP * PAIR
    n_items = 2 * 2 * PAIRS_PER_STEP
    return pl.pallas_call(
        _attn_kernel,
        grid=(bsz, d // width),
        in_specs=[pl.BlockSpec((None, width, seq), lambda b, g: (b, g, 0)),
                  pl.BlockSpec((seq, width), lambda b, g: (b, g)),
                  pl.BlockSpec((None, width, seq), lambda b, g: (b, g, 0))],
        out_specs=pl.BlockSpec((seq, width), lambda b, g: (b, g)),
        out_shape=jax.ShapeDtypeStruct((m, d), BF16),
        scratch_shapes=[pltpu.VMEM((width, Q_BLOCK), F32),
                        pltpu.VMEM((2, n_items, K_BLOCK, Q_BLOCK), F32),
                        pltpu.VMEM((2, n_items, K_BLOCK, Q_BLOCK), BF16)],
        compiler_params=_params(2),
        name="stickbreak",
    )(qt, k, vt)


def kernel(x, norm_pre, norm_post, ffn_w_gate, ffn_w_up, ffn_w_down, a_w_in, a_ln_g, a_w_s,
           a_b_s, a_w_out, b_w_qkv, b_w_o):
    bsz, seq, d = x.shape
    depth = norm_pre.shape[0]
    assert seq % TOKEN_TILE == 0 and (bsz * seq) % FFN_TOKEN_TILE == 0, (bsz, seq)
    assert TOKEN_TILE % (2 * CHUNK) == 0 and TOKEN_TILE % Q_BLOCK == 0
    assert d % (PAIRS_PER_STEP * PAIR) == 0 and ffn_w_gate.shape[-1] % FF_CHUNK == 0
    assert a_w_out.shape[1] == A_GROUPS * LANES and a_w_s.shape[-1] == CHUNK
    h = x.reshape(bsz * seq, d)
    wg, wu, wd = (w.astype(BF16) for w in (ffn_w_gate, ffn_w_up, ffn_w_down))
    a_in, a_out = a_w_in.astype(BF16), a_w_out.astype(BF16)
    b_qkv, b_o = b_w_qkv.astype(BF16), b_w_o.astype(BF16)
    gdim = a_w_out.shape[1] // A_GROUPS
    for layer in range(depth):
        pre = norm_pre[layer][:, None, :]
        post = norm_post[layer][:, None, :]
        h = _ffn(h, pre[0], post[0], wg[layer, 0], wu[layer, 0], wd[layer, 0])
        attn_out = None
        if layer % 2 == 0:
            ia = layer // 2
            bias = jnp.repeat(a_b_s[ia].T, gdim, axis=1)
            h = _mixer_a(h, pre[1], post[1], a_in[ia], a_ln_g[ia][None, :], a_w_s[ia], bias,
                         a_out[ia])
        else:
            ib = layer // 2
            w = b_qkv[ib]
            qt, k, vt = _qkv(h, pre[1], w[:, :d].T, w[:, d:2 * d], w[:, 2 * d:].T, bsz, seq)
            attn_out = (_attention(qt, k, vt, bsz, seq), post[1], b_o[ib])
        h = _ffn(h, pre[2], post[2], wg[layer, 1], wu[layer, 1], wd[layer, 1], attn_out)
    return h.reshape(bsz, seq, d)
```

```python
import math

import jax
import jax.numpy as jnp
from jax import lax
from jax.experimental import pallas as pl
from jax.experimental.pallas import tpu as pltpu

RMS_EPS = 1e-6
LN_EPS = 1e-5
CHUNK = 128
A_GROUPS = 8
HEAD_DIM = 64

LANES = 128
MXU_DIM = 256
V7X_VMEM_BYTES = 64 * 1024 * 1024
VMEM_LIMIT_BYTES = V7X_VMEM_BYTES * 7 // 8

PAIR = 2 * HEAD_DIM
PAIRS_PER_STEP = 2
K_BLOCK = LANES
Q_BLOCK = MXU_DIM
FF_CHUNK = MXU_DIM
TOKEN_TILE = 1024
FFN_TOKEN_TILE = 1024
FFN_ROW_BLOCK = MXU_DIM

SOFTPLUS_CLAMP = 80.0

F32 = jnp.float32
BF16 = jnp.bfloat16


def _rms(x, g):
    ms = jnp.mean(x * x, axis=-1, keepdims=True)
    return x * lax.rsqrt(ms + RMS_EPS) * g


def _dot(a, b):
    return jnp.dot(a, b, preferred_element_type=F32)


def _dot_nt(a, b):
    return lax.dot_general(a, b, (((1,), (1,)), ((), ())), preferred_element_type=F32)


def _params(grid_rank=1):
    return pltpu.CompilerParams(
        dimension_semantics=("parallel",) * grid_rank, vmem_limit_bytes=VMEM_LIMIT_BYTES)


def _resident():
    return pl.BlockSpec(memory_space=pltpu.VMEM)


def _row_spec(d):
    return pl.BlockSpec((1, d), lambda *_: (0, 0))


def _ffn_rows(x, rows, pre_ref, post_ref, wg_ref, wu_ref, wd_ref, o_ref, act_ref):
    d_ff = wg_ref.shape[1]
    xn = _rms(x, pre_ref[...]).astype(BF16)
    for c in range(d_ff // FF_CHUNK):
        sl = slice(c * FF_CHUNK, (c + 1) * FF_CHUNK)
        g = _dot(xn, wg_ref[:, sl])
        u = _dot(xn, wu_ref[:, sl])
        act_ref[rows, sl] = (g * jax.nn.sigmoid(g) * u).astype(BF16)
    f = _dot(act_ref[rows, :], wd_ref[...])
    o_ref[rows, :] = x + 0.5 * _rms(f, post_ref[...])


def _row_blocks(n_rows):
    return [slice(r, r + FFN_ROW_BLOCK) for r in range(0, n_rows, FFN_ROW_BLOCK)]


def _ffn_kernel(h_ref, pre_ref, post_ref, wg_ref, wu_ref, wd_ref, o_ref, act_ref):
    for rows in _row_blocks(h_ref.shape[0]):
        _ffn_rows(h_ref[rows, :], rows, pre_ref, post_ref, wg_ref, wu_ref, wd_ref, o_ref, act_ref)


def _attn_out_ffn_kernel(h_ref, a_ref, mix_post_ref, wo_ref, pre_ref, post_ref, wg_ref, wu_ref,
                         wd_ref, o_ref, act_ref):
    def mixed(rows):
        return h_ref[rows, :] + _rms(_dot(a_ref[rows, :], wo_ref[...]), mix_post_ref[...])

    blocks = _row_blocks(h_ref.shape[0])
    x_next = mixed(blocks[0])
    for r, rows in enumerate(blocks):
        x = x_next
        if r + 1 < len(blocks):
            x_next = mixed(blocks[r + 1])
        _ffn_rows(x, rows, pre_ref, post_ref, wg_ref, wu_ref, wd_ref, o_ref, act_ref)


def _ffn(h, pre_g, post_g, wg, wu, wd, attn_out=None):
    m, d = h.shape
    d_ff = wg.shape[1]
    tm = FFN_TOKEN_TILE
    tile = pl.BlockSpec((tm, d), lambda i: (i, 0))
    ffn_specs = [_row_spec(d), _row_spec(d), _resident(), _resident(), _resident()]
    if attn_out is None:
        body, head_specs, head_args = _ffn_kernel, [tile], (h,)
    else:
        body, head_specs = _attn_out_ffn_kernel, [tile, tile, _row_spec(d), _resident()]
        head_args = (h,) + tuple(attn_out)
    return pl.pallas_call(
        body,
        grid=(m // tm,),
        in_specs=head_specs + ffn_specs,
        out_specs=tile,
        out_shape=jax.ShapeDtypeStruct((m, d), F32),
        scratch_shapes=[pltpu.VMEM((tm, d_ff), BF16)],
        compiler_params=_params(),
        name="ffn" if attn_out is None else "attn_out_ffn",
    )(*head_args, pre_g, post_g, wg, wu, wd)


def _gelu(x):
    return 0.5 * x * (1.0 + lax.erf(x * math.sqrt(0.5)))


def _mixa_kernel(h_ref, pre_ref, post_ref, win_ref, lng_ref, ws_ref, bias_ref, wout_ref,
                 o_ref, uv_ref, u_ref, v_ref, y_ref):
    inner = wout_ref.shape[0]
    gdim = inner // A_GROUPS
    t_idx = lax.broadcasted_iota(jnp.int32, (CHUNK, CHUNK), 0)
    s_idx = lax.broadcasted_iota(jnp.int32, (CHUNK, CHUNK), 1)
    causal = s_idx <= t_idx
    w_s = [jnp.where(causal, ws_ref[g], 0.0).astype(BF16) for g in range(A_GROUPS)]
    def project_in(rows):
        xn = _rms(h_ref[rows, :], pre_ref[...]).astype(BF16)
        uv_ref[rows, :] = _dot(xn, win_ref[...])

    starts = list(range(0, h_ref.shape[0], 2 * CHUNK))
    project_in(slice(0, 2 * CHUNK))
    for r in starts:
        rows = slice(r, r + 2 * CHUNK)
        halves = [slice(r, r + CHUNK), slice(r + CHUNK, r + 2 * CHUNK)]
        if r != starts[-1]:
            project_in(slice(r + 2 * CHUNK, r + 4 * CHUNK))
        x = h_ref[rows, :]
        u_ref[rows, :] = _gelu(uv_ref[rows, :inner])
        v = _gelu(uv_ref[rows, inner:])
        mu = jnp.mean(v, axis=-1, keepdims=True)
        vc = v - mu
        var = jnp.mean(vc * vc, axis=-1, keepdims=True)
        v_ref[rows, :] = (vc * lax.rsqrt(var + LN_EPS) * lng_ref[...]).astype(BF16)
        for g in range(A_GROUPS):
            cols = slice(g * gdim, (g + 1) * gdim)
            v_pair = jnp.concatenate([v_ref[half, cols] for half in halves], axis=1)
            gated = _dot(w_s[g], v_pair)
            for c, half in enumerate(halves):
                gate = gated[:, c * gdim:(c + 1) * gdim] + bias_ref[:, cols]
                y_ref[half, cols] = (u_ref[half, cols] * gate).astype(BF16)
        mix = _dot(y_ref[rows, :], wout_ref[...])
        o_ref[rows, :] = x + _rms(mix, post_ref[...])


def _mixer_a(h, pre_g, post_g, w_in, ln_g, w_s, bias, w_out):
    m, d = h.shape
    inner = w_out.shape[0]
    tm = TOKEN_TILE
    tile = pl.BlockSpec((tm, d), lambda i: (i, 0))
    return pl.pallas_call(
        _mixa_kernel,
        grid=(m // tm,),
        in_specs=[tile, _row_spec(d), _row_spec(d), _resident(), _row_spec(inner),
                  _resident(), _resident(), _resident()],
        out_specs=tile,
        out_shape=jax.ShapeDtypeStruct((m, d), F32),
        scratch_shapes=[pltpu.VMEM((tm, 2 * inner), F32), pltpu.VMEM((tm, inner), F32),
                        pltpu.VMEM((tm, inner), BF16), pltpu.VMEM((tm, inner), BF16)],
        compiler_params=_params(),
        name="mixer_a",
    )(h, pre_g, post_g, w_in, ln_g, w_s, bias, w_out)


def _qkv_kernel(h_ref, pre_ref, wqt_ref, wk_ref, wvt_ref, qt_ref, k_ref, vt_ref):
    for rows in _row_blocks(h_ref.shape[0]):
        xn = _rms(h_ref[rows, :], pre_ref[...]).astype(BF16)
        qt_ref[:, rows] = (_dot_nt(wqt_ref[...], xn) * (HEAD_DIM ** -0.5)).astype(BF16)
        k_ref[rows, :] = _dot(xn, wk_ref[...]).astype(BF16)
        vt_ref[:, rows] = _dot_nt(wvt_ref[...], xn).astype(BF16)


def _qkv(h, pre_g, wqt, wk, wvt, bsz, seq):
    m, d = h.shape
    tm = TOKEN_TILE
    nt = seq // tm
    tile = pl.BlockSpec((tm, d), lambda b, i: (b * nt + i, 0))
    tile_t = pl.BlockSpec((None, d, tm), lambda b, i: (b, 0, i))
    return pl.pallas_call(
        _qkv_kernel,
        grid=(bsz, nt),
        in_specs=[tile, _row_spec(d), _resident(), _resident(), _resident()],
        out_specs=[tile_t, tile, tile_t],
        out_shape=[jax.ShapeDtypeStruct((bsz, d, seq), BF16), jax.ShapeDtypeStruct((m, d), BF16),
                   jax.ShapeDtypeStruct((bsz, d, seq), BF16)],
        compiler_params=_params(2),
        name="qkv",
    )(h, pre_g, wqt, wk, wvt)


def _attn_kernel(qt_ref, k_ref, vt_ref, o_ref, acc_ref, z_ref, a_ref):
    n_heads = 2 * PAIRS_PER_STEP
    row = lax.broadcasted_iota(jnp.int32, (K_BLOCK, Q_BLOCK), 0)
    col = lax.broadcasted_iota(jnp.int32, (K_BLOCK, Q_BLOCK), 1)
    tri_r = lax.broadcasted_iota(jnp.int32, (K_BLOCK, K_BLOCK), 0)
    tri_c = lax.broadcasted_iota(jnp.int32, (K_BLOCK, K_BLOCK), 1)
    upper_strict = jnp.where(tri_r < tri_c, 1.0, 0.0).astype(BF16)
    lane = lax.broadcasted_iota(jnp.int32, (K_BLOCK, PAIR), 1)
    first_head = lane < HEAD_DIM
    items = [(b, hd) for b in range(2) for hd in range(n_heads)]

    def queries(qi):
        return pl.ds(pl.multiple_of(qi * Q_BLOCK, Q_BLOCK), Q_BLOCK)

    def key_rows(qi, n, b):
        j = 2 * (qi - n) + 1 - b
        return pl.ds(pl.multiple_of(j * K_BLOCK, K_BLOCK), K_BLOCK)

    def scores(qi, n, slot):
        for b in range(2):
            for p in range(PAIRS_PER_STEP):
                k_pair = k_ref[key_rows(qi, n, b), p * PAIR:(p + 1) * PAIR]
                zero = jnp.zeros_like(k_pair)
                k_split = jnp.concatenate([jnp.where(first_head, k_pair, zero),
                                           jnp.where(first_head, zero, k_pair)], axis=0)
                z_both = _dot(k_split, qt_ref[p * PAIR:(p + 1) * PAIR, queries(qi)])
                for e in range(2):
                    z_ref[slot, b * n_heads + 2 * p + e] = z_both[e * K_BLOCK:(e + 1) * K_BLOCK]

    def weights(slot, carries, diagonal):
        def seen(b):
            return slice(K_BLOCK, Q_BLOCK) if diagonal and b == 0 else slice(0, Q_BLOCK)

        def strict(b):
            return tri_r < tri_c if b == 0 else row < col

        later, first = {}, {}
        for i, (b, hd) in enumerate(items):
            zz = z_ref[slot, i, :, seen(b)]
            drop = jnp.maximum(zz, jnp.log(1.0 + jnp.exp(jnp.minimum(zz, SOFTPLUS_CLAMP))))
            if diagonal:
                drop = jnp.where(strict(b), drop, 0.0)
            z_ref[slot, i, :, seen(b)] = zz - drop
            first[i] = drop[0:1, :]
            later[i] = _dot(upper_strict, drop.astype(BF16))
        carries = list(carries)
        for i, (b, hd) in enumerate(items):
            w = jnp.exp(z_ref[slot, i, :, seen(b)] - later[i] - carries[hd][:, seen(b)])
            total = later[i][0:1, :] + first[i]
            if diagonal:
                w = jnp.where(strict(b), w, 0.0)
                if b == 0:
                    a_ref[slot, i, :, 0:K_BLOCK] = jnp.zeros((K_BLOCK, K_BLOCK), BF16)
                    total = jnp.concatenate([jnp.zeros((1, K_BLOCK), F32), total], axis=1)
            a_ref[slot, i, :, seen(b)] = w.astype(BF16)
            carries[hd] = carries[hd] + total
        return tuple(carries)

    def accumulate(qi, n, slot):
        for hd in range(n_heads):
            hrows = slice(hd * HEAD_DIM, (hd + 1) * HEAD_DIM)
            total = acc_ref[hrows, :]
            for b in range(2):
                total = total + _dot(vt_ref[hrows, key_rows(qi, n, b)], a_ref[slot, b * n_heads + hd])
            acc_ref[hrows, :] = total

    n_blocks = o_ref.shape[0] // Q_BLOCK

    def query_block(qi, _):
        acc_ref[...] = jnp.zeros_like(acc_ref)
        scores(qi, jnp.minimum(1, qi), 1)
        carries = tuple(jnp.zeros((1, Q_BLOCK), F32) for _ in range(n_heads))
        carries = weights(0, carries, True)

        def pair(n, slot, carries):
            accumulate(qi, n - 1, 1 - slot)
            scores(qi, jnp.minimum(n + 1, qi), 1 - slot)
            return weights(slot, carries, False)

        def two_pairs(i, carries):
            n = 2 * i + 1
            carries = pair(n, 1, carries)
            return lax.cond(n < qi, lambda c: pair(n + 1, 0, c), lambda c: c, carries)

        lax.fori_loop(0, (qi + 1) // 2, two_pairs, carries)
        accumulate(qi, qi, qi % 2)
        scores(jnp.minimum(qi + 1, n_blocks - 1), 0, 0)
        o_ref[queries(qi), :] = acc_ref[...].T.astype(BF16)
        return 0

    scores(0, 0, 0)
    lax.fori_loop(0, n_blocks, query_block, 0)


def _attention(qt, k, vt, bsz, seq):
    m, d = k.shape
    width = PAIRS_PER_STEP * PAIR
    n_items = 2 * 2 * PAIRS_PER_STEP
    return pl.pallas_call(
        _attn_kernel,
        grid=(bsz, d // width),
        in_specs=[pl.BlockSpec((None, width, seq), lambda b, g: (b, g, 0)),
                  pl.BlockSpec((seq, width), lambda b, g: (b, g)),
                  pl.BlockSpec((None, width, seq), lambda b, g: (b, g, 0))],
        out_specs=pl.BlockSpec((seq, width), lambda b, g: (b, g)),
        out_shape=jax.ShapeDtypeStruct((m, d), BF16),
        scratch_shapes=[pltpu.VMEM((width, Q_BLOCK), F32),
                        pltpu.VMEM((2, n_items, K_BLOCK, Q_BLOCK), F32),
                        pltpu.VMEM((2, n_items, K_BLOCK, Q_BLOCK), BF16)],
        compiler_params=_params(2),
        name="stickbreak",
    )(qt, k, vt)


def kernel(x, norm_pre, norm_post, ffn_w_gate, ffn_w_up, ffn_w_down, a_w_in, a_ln_g, a_w_s,
           a_b_s, a_w_out, b_w_qkv, b_w_o):
    bsz, seq, d = x.shape
    depth = norm_pre.shape[0]
    assert seq % TOKEN_TILE == 0 and (bsz * seq) % FFN_TOKEN_TILE == 0, (bsz, seq)
    assert TOKEN_TILE % (2 * CHUNK) == 0 and TOKEN_TILE % Q_BLOCK == 0
    assert d % (PAIRS_PER_STEP * PAIR) == 0 and ffn_w_gate.shape[-1] % FF_CHUNK == 0
    assert a_w_out.shape[1] == A_GROUPS * LANES and a_w_s.shape[-1] == CHUNK
    h = x.reshape(bsz * seq, d)
    wg, wu, wd = (w.astype(BF16) for w in (ffn_w_gate, ffn_w_up, ffn_w_down))
    a_in, a_out = a_w_in.astype(BF16), a_w_out.astype(BF16)
    b_qkv, b_o = b_w_qkv.astype(BF16), b_w_o.astype(BF16)
    gdim = a_w_out.shape[1] // A_GROUPS
    for layer in range(depth):
        pre = norm_pre[layer][:, None, :]
        post = norm_post[layer][:, None, :]
        h = _ffn(h, pre[0], post[0], wg[layer, 0], wu[layer, 0], wd[layer, 0])
        attn_out = None
        if layer % 2 == 0:
            ia = layer // 2
            bias = jnp.repeat(a_b_s[ia].T, gdim, axis=1)
            h = _mixer_a(h, pre[1], post[1], a_in[ia], a_ln_g[ia][None, :], a_w_s[ia], bias,
                         a_out[ia])
        else:
            ib = layer // 2
            w = b_qkv[ib]
            qt, k, vt = _qkv(h, pre[1], w[:, :d].T, w[:, d:2 * d], w[:, 2 * d:].T, bsz, seq)
            attn_out = (_attention(qt, k, vt, bsz, seq), post[1], b_o[ib])
        h = _ffn(h, pre[2], post[2], wg[layer, 1], wu[layer, 1], wd[layer, 1], attn_out)
    return h.reshape(bsz, seq, d)
```

```python
import math

import jax
import jax.numpy as jnp
from jax import lax
from jax.experimental import pallas as pl
from jax.experimental.pallas import tpu as pltpu

RMS_EPS = 1e-6
LN_EPS = 1e-5
CHUNK = 128
A_GROUPS = 8
HEAD_DIM = 64

LANES = 128
MXU_DIM = 256
V7X_VMEM_BYTES = 64 * 1024 * 1024
VMEM_LIMIT_BYTES = V7X_VMEM_BYTES * 7 // 8

PAIR = 2 * HEAD_DIM
PAIRS_PER_STEP = 2
K_BLOCK = LANES
Q_BLOCK = MXU_DIM
FF_CHUNK = MXU_DIM
TOKEN_TILE = 1024
FFN_TOKEN_TILE = 1024
FFN_ROW_BLOCK = MXU_DIM

SOFTPLUS_CLAMP = 80.0

F32 = jnp.float32
BF16 = jnp.bfloat16


def _rms(x, g):
    ms = jnp.mean(x * x, axis=-1, keepdims=True)
    return x * lax.rsqrt(ms + RMS_EPS) * g


def _dot(a, b):
    return jnp.dot(a, b, preferred_element_type=F32)


def _dot_nt(a, b):
    return lax.dot_general(a, b, (((1,), (1,)), ((), ())), preferred_element_type=F32)


def _params(grid_rank=1):
    return pltpu.CompilerParams(
        dimension_semantics=("parallel",) * grid_rank, vmem_limit_bytes=VMEM_LIMIT_BYTES)


def _resident():
    return pl.BlockSpec(memory_space=pltpu.VMEM)


def _row_spec(d):
    return pl.BlockSpec((1, d), lambda *_: (0, 0))


def _ffn_rows(x, rows, pre_ref, post_ref, wg_ref, wu_ref, wd_ref, o_ref, act_ref):
    d_ff = wg_ref.shape[1]
    xn = _rms(x, pre_ref[...]).astype(BF16)
    for c in range(d_ff // FF_CHUNK):
        sl = slice(c * FF_CHUNK, (c + 1) * FF_CHUNK)
        g = _dot(xn, wg_ref[:, sl])
        u = _dot(xn, wu_ref[:, sl])
        act_ref[rows, sl] = (g * jax.nn.sigmoid(g) * u).astype(BF16)
    f = _dot(act_ref[rows, :], wd_ref[...])
    o_ref[rows, :] = x + 0.5 * _rms(f, post_ref[...])


def _row_blocks(n_rows):
    return [slice(r, r + FFN_ROW_BLOCK) for r in range(0, n_rows, FFN_ROW_BLOCK)]


def _ffn_kernel(h_ref, pre_ref, post_ref, wg_ref, wu_ref, wd_ref, o_ref, act_ref):
    for rows in _row_blocks(h_ref.shape[0]):
        _ffn_rows(h_ref[rows, :], rows, pre_ref, post_ref, wg_ref, wu_ref, wd_ref, o_ref, act_ref)


def _attn_out_ffn_kernel(h_ref, a_ref, mix_post_ref, wo_ref, pre_ref, post_ref, wg_ref, wu_ref,
                         wd_ref, o_ref, act_ref):
    def mixed(rows):
        return h_ref[rows, :] + _rms(_dot(a_ref[rows, :], wo_ref[...]), mix_post_ref[...])

    blocks = _row_blocks(h_ref.shape[0])
    x_next = mixed(blocks[0])
    for r, rows in enumerate(blocks):
        x = x_next
        if r + 1 < len(blocks):
            x_next = mixed(blocks[r + 1])
        _ffn_rows(x, rows, pre_ref, post_ref, wg_ref, wu_ref, wd_ref, o_ref, act_ref)


def _ffn(h, pre_g, post_g, wg, wu, wd, attn_out=None):
    m, d = h.shape
    d_ff = wg.shape[1]
    tm = FFN_TOKEN_TILE
    tile = pl.BlockSpec((tm, d), lambda i: (i, 0))
    ffn_specs = [_row_spec(d), _row_spec(d), _resident(), _resident(), _resident()]
    if attn_out is None:
        body, head_specs, head_args = _ffn_kernel, [tile], (h,)
    else:
        body, head_specs = _attn_out_ffn_kernel, [tile, tile, _row_spec(d), _resident()]
        head_args = (h,) + tuple(attn_out)
    return pl.pallas_call(
        body,
        grid=(m // tm,),
        in_specs=head_specs + ffn_specs,
        out_specs=tile,
        out_shape=jax.ShapeDtypeStruct((m, d), F32),
        scratch_shapes=[pltpu.VMEM((tm, d_ff), BF16)],
        compiler_params=_params(),
        name="ffn" if attn_out is None else "attn_out_ffn",
    )(*head_args, pre_g, post_g, wg, wu, wd)


def _gelu(x):
    return 0.5 * x * (1.0 + lax.erf(x * math.sqrt(0.5)))


def _mixa_kernel(h_ref, pre_ref, post_ref, win_ref, lng_ref, ws_ref, bias_ref, wout_ref,
                 o_ref, uv_ref, u_ref, v_ref, y_ref):
    inner = wout_ref.shape[0]
    gdim = inner // A_GROUPS
    t_idx = lax.broadcasted_iota(jnp.int32, (CHUNK, CHUNK), 0)
    s_idx = lax.broadcasted_iota(jnp.int32, (CHUNK, CHUNK), 1)
    causal = s_idx <= t_idx
    w_s = [jnp.where(causal, ws_ref[g], 0.0).astype(BF16) for g in range(A_GROUPS)]
    def project_in(rows):
        xn = _rms(h_ref[rows, :], pre_ref[...]).astype(BF16)
        uv_ref[rows, :] = _dot(xn, win_ref[...])

    starts = list(range(0, h_ref.shape[0], 2 * CHUNK))
    project_in(slice(0, 2 * CHUNK))
    for r in starts:
        rows = slice(r, r + 2 * CHUNK)
        halves = [slice(r, r + CHUNK), slice(r + CHUNK, r + 2 * CHUNK)]
        if r != starts[-1]:
            project_in(slice(r + 2 * CHUNK, r + 4 * CHUNK))
        x = h_ref[rows, :]
        chunks = [slice(c, c + MXU_DIM) for c in range(0, inner, MXU_DIM)]
        total = jnp.zeros((2 * CHUNK, 1), F32)
        for cs in chunks:
            u_ref[rows, cs] = _gelu(uv_ref[rows, cs])
            vs = slice(inner + cs.start, inner + cs.stop)
            v = _gelu(uv_ref[rows, vs])
            uv_ref[rows, vs] = v
            total = total + jnp.sum(v, axis=-1, keepdims=True)
        mu = total / inner
        sq = jnp.zeros((2 * CHUNK, 1), F32)
        for cs in chunks:
            vc = uv_ref[rows, inner + cs.start:inner + cs.stop] - mu
            sq = sq + jnp.sum(vc * vc, axis=-1, keepdims=True)
        scale = lax.rsqrt(sq / inner + LN_EPS)
        for cs in chunks:
            vc = uv_ref[rows, inner + cs.start:inner + cs.stop] - mu
            v_ref[rows, cs] = (vc * scale * lng_ref[:, cs]).astype(BF16)
        for g in range(A_GROUPS):
            cols = slice(g * gdim, (g + 1) * gdim)
            v_pair = jnp.concatenate([v_ref[half, cols] for half in halves], axis=1)
            gated = _dot(w_s[g], v_pair)
            for c, half in enumerate(halves):
                gate = gated[:, c * gdim:(c + 1) * gdim] + bias_ref[:, cols]
                y_ref[half, cols] = (u_ref[half, cols] * gate).astype(BF16)
        mix = _dot(y_ref[rows, :], wout_ref[...])
        o_ref[rows, :] = x + _rms(mix, post_ref[...])


def _mixer_a(h, pre_g, post_g, w_in, ln_g, w_s, bias, w_out):
    m, d = h.shape
    inner = w_out.shape[0]
    tm = TOKEN_TILE
    tile = pl.BlockSpec((tm, d), lambda i: (i, 0))
    return pl.pallas_call(
        _mixa_kernel,
        grid=(m // tm,),
        in_specs=[tile, _row_spec(d), _row_spec(d), _resident(), _row_spec(inner),
                  _resident(), _resident(), _resident()],
        out_specs=tile,
        out_shape=jax.ShapeDtypeStruct((m, d), F32),
        scratch_shapes=[pltpu.VMEM((tm, 2 * inner), F32), pltpu.VMEM((tm, inner), F32),
                        pltpu.VMEM((tm, inner), BF16), pltpu.VMEM((tm, inner), BF16)],
        compiler_params=_params(),
        name="mixer_a",
    )(h, pre_g, post_g, w_in, ln_g, w_s, bias, w_out)


def _qkv_kernel(h_ref, pre_ref, wqt_ref, wk_ref, wvt_ref, qt_ref, k_ref, vt_ref):
    for rows in _row_blocks(h_ref.shape[0]):
        xn = _rms(h_ref[rows, :], pre_ref[...]).astype(BF16)
        qt_ref[:, rows] = (_dot_nt(wqt_ref[...], xn) * (HEAD_DIM ** -0.5)).astype(BF16)
        k_ref[rows, :] = _dot(xn, wk_ref[...]).astype(BF16)
        vt_ref[:, rows] = _dot_nt(wvt_ref[...], xn).astype(BF16)


def _qkv(h, pre_g, wqt, wk, wvt, bsz, seq):
    m, d = h.shape
    tm = TOKEN_TILE
    nt = seq // tm
    tile = pl.BlockSpec((tm, d), lambda b, i: (b * nt + i, 0))
    tile_t = pl.BlockSpec((None, d, tm), lambda b, i: (b, 0, i))
    return pl.pallas_call(
        _qkv_kernel,
        grid=(bsz, nt),
        in_specs=[tile, _row_spec(d), _resident(), _resident(), _resident()],
        out_specs=[tile_t, tile, tile_t],
        out_shape=[jax.ShapeDtypeStruct((bsz, d, seq), BF16), jax.ShapeDtypeStruct((m, d), BF16),
                   jax.ShapeDtypeStruct((bsz, d, seq), BF16)],
        compiler_params=_params(2),
        name="qkv",
    )(h, pre_g, wqt, wk, wvt)


def _attn_kernel(qt_ref, k_ref, vt_ref, o_ref, acc_ref, z_ref, a_ref):
    n_heads = 2 * PAIRS_PER_STEP
    row = lax.broadcasted_iota(jnp.int32, (K_BLOCK, Q_BLOCK), 0)
    col = lax.broadcasted_iota(jnp.int32, (K_BLOCK, Q_BLOCK), 1)
    tri_r = lax.broadcasted_iota(jnp.int32, (K_BLOCK, K_BLOCK), 0)
    tri_c = lax.broadcasted_iota(jnp.int32, (K_BLOCK, K_BLOCK), 1)
    upper_strict = jnp.where(tri_r < tri_c, 1.0, 0.0).astype(BF16)
    lane = lax.broadcasted_iota(jnp.int32, (K_BLOCK, PAIR), 1)
    first_head = lane < HEAD_DIM
    items = [(b, hd) for b in range(2) for hd in range(n_heads)]

    def queries(qi):
        return pl.ds(pl.multiple_of(qi * Q_BLOCK, Q_BLOCK), Q_BLOCK)

    def key_rows(qi, n, b):
        j = 2 * (qi - n) + 1 - b
        return pl.ds(pl.multiple_of(j * K_BLOCK, K_BLOCK), K_BLOCK)

    def scores(qi, n, slot):
        for b in range(2):
            for p in range(PAIRS_PER_STEP):
                k_pair = k_ref[key_rows(qi, n, b), p * PAIR:(p + 1) * PAIR]
                zero = jnp.zeros_like(k_pair)
                k_split = jnp.concatenate([jnp.where(first_head, k_pair, zero),
                                           jnp.where(first_head, zero, k_pair)], axis=0)
                z_both = _dot(k_split, qt_ref[p * PAIR:(p + 1) * PAIR, queries(qi)])
                for e in range(2):
                    z_ref[slot, b * n_heads + 2 * p + e] = z_both[e * K_BLOCK:(e + 1) * K_BLOCK]

    def weights(slot, carries, diagonal):
        def seen(b):
            return slice(K_BLOCK, Q_BLOCK) if diagonal and b == 0 else slice(0, Q_BLOCK)

        def strict(b):
            return tri_r < tri_c if b == 0 else row < col

        later, first = {}, {}
        for i, (b, hd) in enumerate(items):
            zz = z_ref[slot, i, :, seen(b)]
            drop = jnp.maximum(zz, jnp.log(1.0 + jnp.exp(jnp.minimum(zz, SOFTPLUS_CLAMP))))
            if diagonal:
                drop = jnp.where(strict(b), drop, 0.0)
            z_ref[slot, i, :, seen(b)] = zz - drop
            first[i] = drop[0:1, :]
            later[i] = _dot(upper_strict, drop.astype(BF16))
        carries = list(carries)
        for i, (b, hd) in enumerate(items):
            w = jnp.exp(z_ref[slot, i, :, seen(b)] - later[i] - carries[hd][:, seen(b)])
            total = later[i][0:1, :] + first[i]
            if diagonal:
                w = jnp.where(strict(b), w, 0.0)
                if b == 0:
                    a_ref[slot, i, :, 0:K_BLOCK] = jnp.zeros((K_BLOCK, K_BLOCK), BF16)
                    total = jnp.concatenate([jnp.zeros((1, K_BLOCK), F32), total], axis=1)
            a_ref[slot, i, :, seen(b)] = w.astype(BF16)
            carries[hd] = carries[hd] + total
        return tuple(carries)

    def accumulate(qi, n, slot):
        for hd in range(n_heads):
            hrows = slice(hd * HEAD_DIM, (hd + 1) * HEAD_DIM)
            total = acc_ref[hrows, :]
            for b in range(2):
                total = total + _dot(vt_ref[hrows, key_rows(qi, n, b)], a_ref[slot, b * n_heads + hd])
            acc_ref[hrows, :] = total

    n_blocks = o_ref.shape[0] // Q_BLOCK

    def query_block(qi, _):
        acc_ref[...] = jnp.zeros_like(acc_ref)
        scores(qi, jnp.minimum(1, qi), 1)
        carries = tuple(jnp.zeros((1, Q_BLOCK), F32) for _ in range(n_heads))
        carries = weights(0, carries, True)

        def pair(n, slot, carries):
            accumulate(qi, n - 1, 1 - slot)
            scores(qi, jnp.minimum(n + 1, qi), 1 - slot)
            return weights(slot, carries, False)

        def two_pairs(i, carries):
            n = 2 * i + 1
            carries = pair(n, 1, carries)
            return lax.cond(n < qi, lambda c: pair(n + 1, 0, c), lambda c: c, carries)

        lax.fori_loop(0, (qi + 1) // 2, two_pairs, carries)
        accumulate(qi, qi, qi % 2)
        scores(jnp.minimum(qi + 1, n_blocks - 1), 0, 0)
        o_ref[queries(qi), :] = acc_ref[...].T.astype(BF16)
        return 0

    scores(0, 0, 0)
    lax.fori_loop(0, n_blocks, query_block, 0)


def _attention(qt, k, vt, bsz, seq):
    m, d = k.shape
    width = PAIRS_PER_STEP * PAIR
    n_items = 2 * 2 * PAIRS_PER_STEP
    return pl.pallas_call(
        _attn_kernel,
        grid=(bsz, d // width),
        in_specs=[pl.BlockSpec((None, width, seq), lambda b, g: (b, g, 0)),
                  pl.BlockSpec((seq, width), lambda b, g: (b, g)),
                  pl.BlockSpec((None, width, seq), lambda b, g: (b, g, 0))],
        out_specs=pl.BlockSpec((seq, width), lambda b, g: (b, g)),
        out_shape=jax.ShapeDtypeStruct((m, d), BF16),
        scratch_shapes=[pltpu.VMEM((width, Q_BLOCK), F32),
                        pltpu.VMEM((2, n_items, K_BLOCK, Q_BLOCK), F32),
                        pltpu.VMEM((2, n_items, K_BLOCK, Q_BLOCK), BF16)],
        compiler_params=_params(2),
        name="stickbreak",
    )(qt, k, vt)


def kernel(x, norm_pre, norm_post, ffn_w_gate, ffn_w_up, ffn_w_down, a_w_in, a_ln_g, a_w_s,
           a_b_s, a_w_out, b_w_qkv, b_w_o):
    bsz, seq, d = x.shape
    depth = norm_pre.shape[0]
    assert seq % TOKEN_TILE == 0 and (bsz * seq) % FFN_TOKEN_TILE == 0, (bsz, seq)
    assert TOKEN_TILE % (2 * CHUNK) == 0 and TOKEN_TILE % Q_BLOCK == 0
    assert d % (PAIRS_PER_STEP * PAIR) == 0 and ffn_w_gate.shape[-1] % FF_CHUNK == 0
    assert a_w_out.shape[1] == A_GROUPS * LANES and a_w_s.shape[-1] == CHUNK
    h = x.reshape(bsz * seq, d)
    wg, wu, wd = (w.astype(BF16) for w in (ffn_w_gate, ffn_w_up, ffn_w_down))
    a_in, a_out = a_w_in.astype(BF16), a_w_out.astype(BF16)
    b_qkv, b_o = b_w_qkv.astype(BF16), b_w_o.astype(BF16)
    gdim = a_w_out.shape[1] // A_GROUPS
    for layer in range(depth):
        pre = norm_pre[layer][:, None, :]
        post = norm_post[layer][:, None, :]
        h = _ffn(h, pre[0], post[0], wg[layer, 0], wu[layer, 0], wd[layer, 0])
        attn_out = None
        if layer % 2 == 0:
            ia = layer // 2
            bias = jnp.repeat(a_b_s[ia].T, gdim, axis=1)
            h = _mixer_a(h, pre[1], post[1], a_in[ia], a_ln_g[ia][None, :], a_w_s[ia], bias,
                         a_out[ia])
        else:
            ib = layer // 2
            w = b_qkv[ib]
            qt, k, vt = _qkv(h, pre[1], w[:, :d].T, w[:, d:2 * d], w[:, 2 * d:].T, bsz, seq)
            attn_out = (_attention(qt, k, vt, bsz, seq), post[1], b_o[ib])
        h = _ffn(h, pre[2], post[2], wg[layer, 1], wu[layer, 1], wd[layer, 1], attn_out)
    return h.reshape(bsz, seq, d)
```
